```python
import math
import jax, jax.numpy as jnp
from jax import lax
import numpy as np

D_MODEL = 1024
BATCH = 8
SEQ = 4096
DEPTH = 1

CHUNK = 64
QBLOCK = 128
ROPE_THETA = 10000.0
RMS_EPS = 1e-6

SB_HEADS = 8
SB_HEAD_DIM = 64
SB_WIDTH = SB_HEADS * SB_HEAD_DIM

DSA_HEADS = 8
DSA_HEAD_DIM = 64
DSA_WIDTH = DSA_HEADS * DSA_HEAD_DIM
IDX_HEADS = 4
IDX_DIM = 64
MAX_TOPK = 256

MEM_TOKENS = 256
MEM_HEADS = 4
MEM_HEAD_DIM = 128
MEM_WIDTH = MEM_HEADS * MEM_HEAD_DIM

N_BRANCHES = 3

SPLIT_SIZES = (
    SB_WIDTH, SB_WIDTH, SB_WIDTH, SB_WIDTH,
    DSA_WIDTH, DSA_WIDTH, DSA_WIDTH, DSA_WIDTH,
    IDX_HEADS * IDX_DIM, IDX_DIM, IDX_HEADS,
    MEM_WIDTH, MEM_WIDTH,
    N_BRANCHES * D_MODEL,
)
IN_WIDTH = 4 * SB_WIDTH + 4 * DSA_WIDTH + IDX_HEADS * IDX_DIM + IDX_DIM + IDX_HEADS + 2 * MEM_WIDTH + N_BRANCHES * D_MODEL

kernel_name = "hybrid_stickbreak_dsa_memxattn_block"


def rmsnorm(x, g):
    xf = x.astype(jnp.float32)
    out = xf * lax.rsqrt(jnp.mean(xf * xf, axis=-1, keepdims=True) + RMS_EPS) * g.astype(jnp.float32)
    return out.astype(x.dtype)


def split_heads(t, n, d):
    return t.reshape(t.shape[0], t.shape[1], n, d)


def rope(t, positions):
    d = t.shape[-1]
    half = d // 2
    inv = ROPE_THETA ** (-jnp.arange(half, dtype=jnp.float32) / half)
    ang = positions.astype(jnp.float32)[..., None] * inv
    if t.ndim == 4:
        ang = ang[:, :, None, :]
    cos, sin = jnp.cos(ang), jnp.sin(ang)
    tf = t.astype(jnp.float32)
    t1, t2 = tf[..., :half], tf[..., half:]
    return jnp.concatenate([t1 * cos - t2 * sin, t2 * cos + t1 * sin], axis=-1).astype(t.dtype)


def stick_breaking_attention(q, k, v):
    S = q.shape[1]
    scale = q.shape[-1] ** -0.5
    outs = []
    for i in range(S // QBLOCK):
        q0 = i * QBLOCK
        kl = q0 + QBLOCK
        z = jnp.einsum('bqhd,bkhd->bhqk', q[:, q0:kl], k[:, :kl]).astype(jnp.float32) * scale
        t_idx = q0 + jnp.arange(QBLOCK)[:, None]
        s_idx = jnp.arange(kl)[None, :]
        causal = s_idx < t_idx
        log_beta = jax.nn.log_sigmoid(z)
        log_1m = jnp.where(causal, jax.nn.log_sigmoid(-z), 0.0)
        suffix = lax.cumsum(log_1m, axis=3, reverse=True) - log_1m
        a = jnp.where(causal, jnp.exp(log_beta + suffix), 0.0)
        outs.append(jnp.einsum('bhqk,bkhd->bqhd', a.astype(v.dtype), v[:, :kl]))
    return jnp.concatenate(outs, axis=1)


def gather_rows(arr, idx):
    return jax.vmap(lambda a, i: a[i])(arr, idx)


def dsa_attention(q, k, v, q_idx, k_idx, w_idx, topk):
    S = q.shape[1]
    scale = q.shape[-1] ** -0.5
    idx_scale = IDX_DIM ** -0.5
    outs = []
    for i in range(S // QBLOCK):
        q0 = i * QBLOCK
        kl = q0 + QBLOCK
        t_idx = q0 + jnp.arange(QBLOCK)
        limit = (t_idx // CHUNK + 1) * CHUNK
        admissible = jnp.arange(kl)[None, :] < limit[:, None]
        sc = jnp.einsum('bqhd,bkd->bqhk', q_idx[:, q0:kl], k_idx[:, :kl]).astype(jnp.float32) * idx_scale
        score = jnp.einsum('bqh,bqhk->bqk', w_idx[:, q0:kl].astype(jnp.float32), jax.nn.relu(sc))
        score = jnp.where(admissible[None], score, -jnp.inf)
        kk = min(topk, kl)
        _, sel = lax.top_k(score, kk)
        valid = sel < limit[None, :, None]
        k_sel = gather_rows(k[:, :kl], sel)
        v_sel = gather_rows(v[:, :kl], sel)
        logits = jnp.einsum('bqhd,bqkhd->bhqk', q[:, q0:kl], k_sel).astype(jnp.float32) * scale
        logits = jnp.where(valid[:, None], logits, -jnp.inf)
        p = jax.nn.softmax(logits, axis=-1)
        outs.append(jnp.einsum('bhqk,bqkhd->bqhd', p.astype(v.dtype), v_sel))
    return jnp.concatenate(outs, axis=1)


def memory_cross_attention(q, mem_k, mem_v):
    scale = q.shape[-1] ** -0.5
    logits = jnp.einsum('bshd,bmhd->bhsm', q, mem_k).astype(jnp.float32) * scale
    p = jax.nn.softmax(logits, axis=-1)
    return jnp.einsum('bhsm,bmhd->bshd', p.astype(mem_v.dtype), mem_v)


def hybrid_layer(x, mem, positions, pre_g, post_g, mem_g, w_in, b_gate, w_mem_kv,
                 w_br_a, w_br_b, w_br_c, w_out):
    B, S, _ = x.shape
    topk = min(MAX_TOPK, S // 4)
    h = rmsnorm(x, pre_g)
    proj = h @ w_in
    split_points = [int(p) for p in np.cumsum(SPLIT_SIZES)[:-1]]
    (qa, ka, va, za, qb, kb, vb, zb, qi, ki, wi, qc, zc, gates) = jnp.split(proj, split_points, axis=-1)

    ya = stick_breaking_attention(split_heads(qa, SB_HEADS, SB_HEAD_DIM),
                                  split_heads(ka, SB_HEADS, SB_HEAD_DIM),
                                  split_heads(va, SB_HEADS, SB_HEAD_DIM)).reshape(B, S, SB_WIDTH)
    ya = (ya * jax.nn.silu(za)) @ w_br_a

    qb_h = rope(split_heads(qb, DSA_HEADS, DSA_HEAD_DIM), positions)
    kb_h = rope(split_heads(kb, DSA_HEADS, DSA_HEAD_DIM), positions)
    vb_h = split_heads(vb, DSA_HEADS, DSA_HEAD_DIM)
    qi_h = rope(split_heads(qi, IDX_HEADS, IDX_DIM), positions)
    ki_r = rope(ki, positions)
    wi_s = wi * (IDX_HEADS ** -0.5)
    yb = dsa_attention(qb_h, kb_h, vb_h, qi_h, ki_r, wi_s, topk).reshape(B, S, DSA_WIDTH)
    yb = (yb * jax.nn.silu(zb)) @ w_br_b

    mem_kv = rmsnorm(mem, mem_g) @ w_mem_kv
    mk, mv = jnp.split(mem_kv, 2, axis=-1)
    yc = memory_cross_attention(split_heads(qc, MEM_HEADS, MEM_HEAD_DIM),
                                split_heads(mk, MEM_HEADS, MEM_HEAD_DIM),
                                split_heads(mv, MEM_HEADS, MEM_HEAD_DIM)).reshape(B, S, MEM_WIDTH)
    yc = (yc * jax.nn.silu(zc)) @ w_br_c

    g = jax.nn.sigmoid(gates + b_gate).reshape(B, S, N_BRANCHES, D_MODEL)
    merged = g[:, :, 0] * ya + g[:, :, 1] * yb + g[:, :, 2] * yc
    out = merged @ w_out
    return x + rmsnorm(out, post_g)


def setup_inputs(seed: int = 0) -> dict:
    key = jax.random.key(seed)
    ks = jax.random.split(key, 16)
    f32 = jnp.float32
    x = jax.random.normal(ks[0], (BATCH, SEQ, D_MODEL), f32)
    mem = jax.random.normal(ks[1], (BATCH, MEM_TOKENS, D_MODEL), f32)
    offsets = jax.random.randint(ks[2], (BATCH, 1), 0, 4096, dtype=jnp.int32)
    positions = offsets + jnp.arange(SEQ, dtype=jnp.int32)[None, :]
    pre_g = 1.0 + 0.05 * jax.random.normal(ks[3], (DEPTH, D_MODEL), f32)
    post_g = 1.0 + 0.05 * jax.random.normal(ks[4], (DEPTH, D_MODEL), f32)
    mem_g = 1.0 + 0.05 * jax.random.normal(ks[5], (DEPTH, D_MODEL), f32)
    w_in = jax.random.normal(ks[6], (DEPTH, D_MODEL, IN_WIDTH), f32) * D_MODEL ** -0.5
    b_gate = 0.02 * jax.random.normal(ks[7], (DEPTH, N_BRANCHES * D_MODEL), f32)
    w_mem_kv = jax.random.normal(ks[8], (DEPTH, D_MODEL, 2 * MEM_WIDTH), f32) * D_MODEL ** -0.5
    w_br_a = jax.random.normal(ks[9], (DEPTH, SB_WIDTH, D_MODEL), f32) * SB_WIDTH ** -0.5
    w_br_b = jax.random.normal(ks[10], (DEPTH, DSA_WIDTH, D_MODEL), f32) * DSA_WIDTH ** -0.5
    w_br_c = jax.random.normal(ks[11], (DEPTH, MEM_WIDTH, D_MODEL), f32) * MEM_WIDTH ** -0.5
    w_out = jax.random.normal(ks[12], (DEPTH, D_MODEL, D_MODEL), f32) * D_MODEL ** -0.5
    return {"x": x, "mem": mem, "positions": positions, "pre_g": pre_g, "post_g": post_g,
            "mem_g": mem_g, "w_in": w_in, "b_gate": b_gate, "w_mem_kv": w_mem_kv,
            "w_br_a": w_br_a, "w_br_b": w_br_b, "w_br_c": w_br_c, "w_out": w_out}


def reference(x, mem, positions, pre_g, post_g, mem_g, w_in, b_gate, w_mem_kv,
              w_br_a, w_br_b, w_br_c, w_out):
    for layer in range(DEPTH):
        x = hybrid_layer(x, mem, positions, pre_g[layer], post_g[layer], mem_g[layer],
                         w_in[layer], b_gate[layer], w_mem_kv[layer], w_br_a[layer],
                         w_br_b[layer], w_br_c[layer], w_out[layer])
    return x
```

```python
import functools

import numpy as np
import jax
import jax.numpy as jnp
from jax import lax
from jax.experimental import pallas as pl
from jax.experimental.pallas import tpu as pltpu

F32 = jnp.float32
BF16 = jnp.bfloat16

D_MODEL = 1024
CHUNK = 64
ROPE_THETA = 10000.0
RMS_EPS = 1e-6
HEAD_DIM = 64
HALF = HEAD_DIM // 2
AB_WIDTH = 512
IDX_HEADS = 4
IDX_DIM = 64
MAX_TOPK = 256
MEM_HEADS = 4
MEM_HEAD_DIM = 128
MEM_WIDTH = MEM_HEADS * MEM_HEAD_DIM
N_BRANCHES = 3

LANES = 128
T_BLK = 256
NEG_BIG = -1e30
VMEM_LIMIT = 56 * 1024 * 1024

_NT_DIMS = (((1,), (1,)), ((), ()))


def _dot(a, b):
    return jnp.dot(a, b, preferred_element_type=F32)


def _params(sem, vmem=VMEM_LIMIT):
    return pltpu.CompilerParams(dimension_semantics=sem, vmem_limit_bytes=vmem)


def _rms_kernel(x_ref, g_ref, o_ref):
    x = x_ref[...]
    ms = jnp.mean(x * x, axis=-1, keepdims=True)
    o_ref[...] = (x * lax.rsqrt(ms + RMS_EPS) * g_ref[...]).astype(o_ref.dtype)


def _rmsnorm_bf16(x2d, g, tm=512):
    n, d = x2d.shape
    return pl.pallas_call(
        _rms_kernel,
        out_shape=jax.ShapeDtypeStruct((n, d), BF16),
        grid=(n // tm,),
        in_specs=[pl.BlockSpec((tm, d), lambda i: (i, 0)),
                  pl.BlockSpec((1, d), lambda i: (0, 0))],
        out_specs=pl.BlockSpec((tm, d), lambda i: (i, 0)),
        compiler_params=_params(("parallel",)),
        name="rmsnorm",
    )(x2d, g.reshape(1, d))


def _rope_table_kernel(pos_ref, inv_ref, cos_ref, sin_ref):
    ang = pos_ref[...].astype(F32) * inv_ref[...]
    cos_ref[...] = jnp.cos(ang)
    sin_ref[...] = jnp.sin(ang)


def _rope_tables_nat(pos_rep, inv_row):
    r = pos_rep.shape[0]
    br = min(512, r)
    out = jax.ShapeDtypeStruct((r, LANES), F32)
    return pl.pallas_call(
        _rope_table_kernel,
        out_shape=(out, out),
        grid=(r // br,),
        in_specs=[pl.BlockSpec((br, LANES), lambda i: (i, 0)),
                  pl.BlockSpec((1, LANES), lambda i: (0, 0))],
        out_specs=(pl.BlockSpec((br, LANES), lambda i: (i, 0)),
                   pl.BlockSpec((br, LANES), lambda i: (i, 0))),
        compiler_params=_params(("parallel",)),
        name="rope_tables_nat",
    )(pos_rep, inv_row)


def _rope_table_t_kernel(pos_ref, inv_ref, cos_ref, sin_ref):
    ang = pos_ref[0].astype(F32) * inv_ref[...]
    cos_ref[0] = jnp.cos(ang)
    sin_ref[0] = jnp.sin(ang)


def _rope_tables_tr(pos3, inv_col):
    b, _, s = pos3.shape
    ts = min(512, s)
    out = jax.ShapeDtypeStruct((b, HALF, s), F32)
    return pl.pallas_call(
        _rope_table_t_kernel,
        out_shape=(out, out),
        grid=(b, s // ts),
        in_specs=[pl.BlockSpec((1, 1, ts), lambda i, j: (i, 0, j)),
                  pl.BlockSpec((HALF, 1), lambda i, j: (0, 0))],
        out_specs=(pl.BlockSpec((1, HALF, ts), lambda i, j: (i, 0, j)),
                   pl.BlockSpec((1, HALF, ts), lambda i, j: (i, 0, j))),
        compiler_params=_params(("parallel", "parallel")),
        name="rope_tables_tr",
    )(pos3, inv_col)


def _proj_nat_kernel(h_ref, w_ref, o_ref):
    o_ref[...] = _dot(h_ref[...], w_ref[...]).astype(o_ref.dtype)


def _proj_nat(h, w, tm=512):
    n, d = h.shape
    c = w.shape[1]
    return pl.pallas_call(
        _proj_nat_kernel,
        out_shape=jax.ShapeDtypeStruct((n, c), BF16),
        grid=(n // tm,),
        in_specs=[pl.BlockSpec((tm, d), lambda i: (i, 0)),
                  pl.BlockSpec((d, c), lambda i: (0, 0))],
        out_specs=pl.BlockSpec((tm, c), lambda i: (i, 0)),
        compiler_params=_params(("parallel",)),
        name="proj_nat",
    )(h, w)


def _proj_nat_rope_kernel(h_ref, w_ref, wr_ref, cos_ref, sin_ref, o_ref):
    h = h_ref[...]
    a = _dot(h, w_ref[...])
    b = _dot(h, wr_ref[...])
    c = cos_ref[...]
    s = sin_ref[...]
    for g in range(o_ref.shape[1] // LANES):
        sl = slice(g * LANES, (g + 1) * LANES)
        o_ref[:, sl] = (a[:, sl] * c + b[:, sl] * s).astype(o_ref.dtype)


def _proj_nat_rope(h, w, wr, cos_n, sin_n, tm=512):
    n, d = h.shape
    c = w.shape[1]
    return pl.pallas_call(
        _proj_nat_rope_kernel,
        out_shape=jax.ShapeDtypeStruct((n, c), BF16),
        grid=(n // tm,),
        in_specs=[pl.BlockSpec((tm, d), lambda i: (i, 0)),
                  pl.BlockSpec((d, c), lambda i: (0, 0)),
                  pl.BlockSpec((d, c), lambda i: (0, 0)),
                  pl.BlockSpec((tm, LANES), lambda i: (i, 0)),
                  pl.BlockSpec((tm, LANES), lambda i: (i, 0))],
        out_specs=pl.BlockSpec((tm, c), lambda i: (i, 0)),
        compiler_params=_params(("parallel",)),
        name="proj_nat_rope",
    )(h, w, wr, cos_n, sin_n)


def _proj_tr_kernel(wt_ref, h_ref, o_ref):
    r = lax.dot_general(wt_ref[...], h_ref[...], _NT_DIMS, preferred_element_type=F32)
    o_ref[0, 0] = r.astype(o_ref.dtype)


def _proj_tr(h, wt, batch, seq):
    c, d = wt.shape
    nblk = seq // T_BLK
    return pl.pallas_call(
        _proj_tr_kernel,
        out_shape=jax.ShapeDtypeStruct((batch, nblk, c, T_BLK), BF16),
        grid=(batch, nblk),
        in_specs=[pl.BlockSpec((c, d), lambda b, i: (0, 0)),
                  pl.BlockSpec((T_BLK, d), lambda b, i: (b * nblk + i, 0))],
        out_specs=pl.BlockSpec((1, 1, c, T_BLK), lambda b, i: (b, i, 0, 0)),
        compiler_params=_params(("parallel", "parallel")),
        name="proj_tr",
    )(wt, h)


def _proj_tr_rope_kernel(wt_ref, h_ref, cos_ref, sin_ref, o_ref, w_ref, *, n_rope_heads):
    r = lax.dot_general(wt_ref[...], h_ref[...], _NT_DIMS, preferred_element_type=F32)
    c = cos_ref[0]
    s = sin_ref[0]
    for hd in range(n_rope_heads):
        lo = hd * HEAD_DIM
        t1 = r[lo:lo + HALF]
        t2 = r[lo + HALF:lo + HEAD_DIM]
        o_ref[0, 0, lo:lo + HALF, :] = (t1 * c - t2 * s).astype(o_ref.dtype)
        o_ref[0, 0, lo + HALF:lo + HEAD_DIM, :] = (t2 * c + t1 * s).astype(o_ref.dtype)
    w_ref[0, 0] = r[n_rope_heads * HEAD_DIM:]


def _proj_tr_rope(h, wt, cos_t, sin_t, batch, seq, n_rope_heads):
    c, d = wt.shape
    c_rope = n_rope_heads * HEAD_DIM
    c_rest = c - c_rope
    nblk = seq // T_BLK
    return pl.pallas_call(
        functools.partial(_proj_tr_rope_kernel, n_rope_heads=n_rope_heads),
        out_shape=(jax.ShapeDtypeStruct((batch, nblk, c_rope, T_BLK), BF16),
                   jax.ShapeDtypeStruct((batch, nblk, c_rest, T_BLK), F32)),
        grid=(batch, nblk),
        in_specs=[pl.BlockSpec((c, d), lambda b, i: (0, 0)),
                  pl.BlockSpec((T_BLK, d), lambda b, i: (b * nblk + i, 0)),
                  pl.BlockSpec((1, HALF, T_BLK), lambda b, i: (b, 0, i)),
                  pl.BlockSpec((1, HALF, T_BLK), lambda b, i: (b, 0, i))],
        out_specs=(pl.BlockSpec((1, 1, c_rope, T_BLK), lambda b, i: (b, i, 0, 0)),
                   pl.BlockSpec((1, 1, c_rest, T_BLK), lambda b, i: (b, i, 0, 0))),
        compiler_params=_params(("parallel", "parallel")),
        name="proj_tr_rope",
    )(wt, h, cos_t, sin_t)


def _memkv_kernel(mem_ref, g_ref, w_ref, o_ref):
    x = mem_ref[0]
    ms = jnp.mean(x * x, axis=-1, keepdims=True)
    hn = (x * lax.rsqrt(ms + RMS_EPS) * g_ref[...]).astype(BF16)
    o_ref[0] = _dot(hn, w_ref[...]).astype(o_ref.dtype)


def _memkv(mem, g, w):
    b, m, d = mem.shape
    c = w.shape[1]
    return pl.pallas_call(
        _memkv_kernel,
        out_shape=jax.ShapeDtypeStruct((b, m, c), BF16),
        grid=(b,),
        in_specs=[pl.BlockSpec((1, m, d), lambda i: (i, 0, 0)),
                  pl.BlockSpec((1, d), lambda i: (0, 0)),
                  pl.BlockSpec((d, c), lambda i: (0, 0))],
        out_specs=pl.BlockSpec((1, m, c), lambda i: (i, 0, 0)),
        compiler_params=_params(("parallel",)),
        name="mem_kv",
    )(mem, g.reshape(1, d), w)


def _stick_kernel(q_ref, kT_ref, v_ref, o_ref):
    t = T_BLK
    i = pl.program_id(2)
    q = q_ref[...]
    lane = lax.broadcasted_iota(jnp.int32, (t, LANES), 1)
    zero_b = jnp.zeros((), BF16)
    q_heads = (jnp.where(lane < HEAD_DIM, q, zero_b), jnp.where(lane >= HEAD_DIM, q, zero_b))
    row = lax.broadcasted_iota(jnp.int32, (t, t), 0)
    col = lax.broadcasted_iota(jnp.int32, (t, t), 1)
    tri = jnp.where(row > col, 1.0, 0.0).astype(BF16)
    causal = col < row

    def block(j, carry, diag):
        acc, r0, r1 = carry
        k_t = kT_ref[0, j]
        v = v_ref[pl.ds(pl.multiple_of(j * t, t), t), :]
        v_heads = (jnp.where(lane < HEAD_DIM, v, zero_b), jnp.where(lane >= HEAD_DIM, v, zero_b))
        rs = [r0, r1]
        for hh in range(2):
            z = _dot(q_heads[hh], k_t)
            sp = jnp.log(1.0 + jnp.exp(-jnp.abs(z)))
            lb = jnp.minimum(z, 0.0) - sp
            l1 = lb - z
            if diag:
                l1 = jnp.where(causal, l1, 0.0)
            hi = l1.astype(BF16)
            lo = (l1 - hi.astype(F32)).astype(BF16)
            cum = _dot(hi, tri) + _dot(lo, tri)
            a = jnp.exp(lb + cum + rs[hh])
            if diag:
                a = jnp.where(causal, a, 0.0)
            acc = acc + _dot(a.astype(BF16), v_heads[hh])
            rs[hh] = rs[hh] + jnp.sum(l1, axis=1, keepdims=True)
        return acc, rs[0], rs[1]

    init = (jnp.zeros((t, LANES), F32), jnp.zeros((t, 1), F32), jnp.zeros((t, 1), F32))
    carry = block(i, init, True)
    carry = lax.fori_loop(0, i, lambda jj, c: block(i - 1 - jj, c, False), carry)
    o_ref[...] = carry[0]


def _stick_breaking(nat, tr, batch, seq):
    t = T_BLK
    nblk = seq // t
    npairs = AB_WIDTH // LANES
    n = batch * seq
    return pl.pallas_call(
        _stick_kernel,
        out_shape=jax.ShapeDtypeStruct((n, AB_WIDTH), F32),
        grid=(batch, npairs, nblk),
        in_specs=[pl.BlockSpec((t, LANES), lambda b, p, i: (b * nblk + i, p)),
                  pl.BlockSpec((1, nblk, LANES, t), lambda b, p, i: (b, 0, p, 0)),
                  pl.BlockSpec((seq, LANES), lambda b, p, i: (b, npairs + p))],
        out_specs=pl.BlockSpec((t, LANES), lambda b, p, i: (b * nblk + i, p)),
        compiler_params=_params(("parallel", "parallel", "arbitrary")),
        name="stick_breaking",
    )(nat, tr, nat)


def _key_to_float(key):
    bits = key ^ ((key >> 31) & jnp.int32(0x7FFFFFFF))
    return lax.bitcast_convert_type(bits, F32)


def _dsa_kernel(kn_ref, vT_ref, qT_ref, wi_ref, o_ref, sc_ref, bias_ref, *, topk):
    t = T_BLK
    i = pl.program_id(1)
    nblk = i + 1
    q_all = qT_ref[0, 0]
    wi = wi_ref[0, 0]
    tq = i * t + lax.broadcasted_iota(jnp.int32, (1, t), 1)
    limit = ((tq >> 6) + 1) << 6
    kcount = jnp.minimum(limit, topk).astype(F32)
    krow = lax.broadcasted_iota(jnp.int32, (t, t), 0)
    kcol = lax.broadcasted_iota(jnp.int32, (t, t), 1)
    qi_off = AB_WIDTH

    def rows(j):
        return pl.ds(pl.multiple_of(j * t, t), t)

    def score_body(j, _):
        ki = kn_ref[rows(j), AB_WIDTH:AB_WIDTH + IDX_DIM]
        sc = jnp.zeros((t, t), F32)
        for h in range(IDX_HEADS):
            qh = q_all[qi_off + h * IDX_DIM:qi_off + (h + 1) * IDX_DIM]
            sc = sc + wi[h:h + 1] * jnp.maximum(_dot(ki, qh), 0.0)
        sc_ref[rows(j), :] = jnp.where(j * t + krow < limit, sc, -jnp.inf)
        return 0

    lax.fori_loop(0, nblk, score_body, 0)

    def count(pred):
        def body(j, c):
            return c + jnp.sum(jnp.where(pred(sc_ref[rows(j), :]), 1.0, 0.0), axis=0, keepdims=True)
        return lax.fori_loop(0, nblk, body, jnp.zeros((1, t), F32))

    def search_body(it, p):
        cand = p + lax.shift_left(jnp.int32(1), 31 - it)
        cf = _key_to_float(cand)
        cnt = count(lambda s: s >= cf)
        return jnp.where(cnt >= kcount, cand, p)

    p = lax.fori_loop(0, 32, search_body, jnp.full((1, t), jnp.iinfo(jnp.int32).min, jnp.int32))
    thr = _key_to_float(p)
    n_ties = kcount - count(lambda s: s > thr)

    tri = jnp.where(kcol < krow, 1.0, 0.0).astype(BF16)

    def bias_body(j, seen):
        s = sc_ref[rows(j), :]
        tie = s == thr
        tie_f = jnp.where(tie, 1.0, 0.0)
        before = _dot(tri, tie_f.astype(BF16)) + seen
        keep_tie = jnp.where(before < n_ties, 0.0, NEG_BIG)
        bias_ref[rows(j), :] = jnp.where(s > thr, 0.0, jnp.where(tie, keep_tie, NEG_BIG))
        return seen + jnp.sum(tie_f, axis=0, keepdims=True)

    lax.fori_loop(0, nblk, bias_body, jnp.zeros((1, t), F32))

    lane = lax.broadcasted_iota(jnp.int32, (t, LANES), 1)
    prow = lax.broadcasted_iota(jnp.int32, (LANES, t), 0)
    zero_b = jnp.zeros((), BF16)
    for pr in range(AB_WIDTH // LANES):
        q_pair = q_all[pr * LANES:(pr + 1) * LANES]

        def att_body(j, carry, pr=pr, q_pair=q_pair):
            m0, l0, m1, l1, acc = carry
            kb = kn_ref[rows(j), pr * LANES:(pr + 1) * LANES]
            v_t = vT_ref[0, j, pr * LANES:(pr + 1) * LANES, :]
            bias = bias_ref[rows(j), :]
            ms, ls, alphas, pvs = [m0, m1], [l0, l1], [], []
            for hh in range(2):
                sel_k = (lane < HEAD_DIM) if hh == 0 else (lane >= HEAD_DIM)
                sel_v = (prow < HEAD_DIM) if hh == 0 else (prow >= HEAD_DIM)
                s = _dot(jnp.where(sel_k, kb, zero_b), q_pair) + bias
                m_new = jnp.maximum(ms[hh], jnp.max(s, axis=0, keepdims=True))
                alpha = jnp.exp(ms[hh] - m_new)
                pm = jnp.exp(s - m_new)
                ls[hh] = alpha * ls[hh] + jnp.sum(pm, axis=0, keepdims=True)
                ms[hh] = m_new
                alphas.append(alpha)
                pvs.append(_dot(jnp.where(sel_v, v_t, zero_b), pm.astype(BF16)))
            acc = acc * jnp.where(prow < HEAD_DIM, alphas[0], alphas[1]) + pvs[0] + pvs[1]
            return ms[0], ls[0], ms[1], ls[1], acc

        neg = jnp.full((1, t), NEG_BIG, F32)
        zero = jnp.zeros((1, t), F32)
        m0, l0, m1, l1, acc = lax.fori_loop(
            0, nblk, att_body, (neg, zero, neg, zero, jnp.zeros((LANES, t), F32)))
        out_t = acc / jnp.where(prow < HEAD_DIM, l0, l1)
        o_ref[:, pr * LANES:(pr + 1) * LANES] = out_t.T


def _dsa(kn, tr, q_t, wi_t, batch, seq, topk):
    t = T_BLK
    nblk = seq // t
    n = batch * seq
    ckn = kn.shape[1]
    cq = q_t.shape[2]
    return pl.pallas_call(
        functools.partial(_dsa_kernel, topk=topk),
        out_shape=jax.ShapeDtypeStruct((n, AB_WIDTH), F32),
        grid=(batch, nblk),
        in_specs=[pl.BlockSpec((seq, ckn), lambda b, i: (b, 0)),
                  pl.BlockSpec((1, nblk, AB_WIDTH, t), lambda b, i: (b, 0, 1, 0)),
                  pl.BlockSpec((1, 1, cq, t), lambda b, i: (b, i, 0, 0)),
                  pl.BlockSpec((1, 1, 8, t), lambda b, i: (b, i, 0, 0))],
        out_specs=pl.BlockSpec((t, AB_WIDTH), lambda b, i: (b * nblk + i, 0)),
        scratch_shapes=[pltpu.VMEM((seq, t), F32), pltpu.VMEM((seq, t), F32)],
        compiler_params=_params(("parallel", "arbitrary")),
        name="dsa",
    )(kn, tr, q_t, wi_t)


def _sigmoid(x):
    return 1.0 / (1.0 + jnp.exp(-x))


def _final_kernel(x_ref, h_ref, ya_ref, yb_ref, mkv_ref, wz_ref, wg_ref, wqc_ref, bg_ref,
                  wa_ref, wb_ref, wc_ref, wo_ref, pg_ref, o_ref):
    h = h_ref[...]
    w = AB_WIDTH

    def branch(y, zi, w_ref, gi):
        z = _dot(h, wz_ref[:, zi * w:(zi + 1) * w])
        u = (y * (z * _sigmoid(z))).astype(BF16)
        g = _sigmoid(_dot(h, wg_ref[:, gi * D_MODEL:(gi + 1) * D_MODEL])
                     + bg_ref[:, gi * D_MODEL:(gi + 1) * D_MODEL])
        return g * _dot(u, w_ref[...])

    merged = branch(ya_ref[...], 0, wa_ref, 0)
    merged = merged + branch(yb_ref[...], 1, wb_ref, 1)

    qc = _dot(h, wqc_ref[...]).astype(BF16)
    mkv = mkv_ref[0]
    scale = MEM_HEAD_DIM ** -0.5
    heads = []
    for hd in range(MEM_HEADS):
        sl = slice(hd * MEM_HEAD_DIM, (hd + 1) * MEM_HEAD_DIM)
        logits = lax.dot_general(qc[:, sl], mkv[:, sl], _NT_DIMS,
                                 preferred_element_type=F32) * scale
        mx = jnp.max(logits, axis=-1, keepdims=True)
        e = jnp.exp(logits - mx)
        pm = e / jnp.sum(e, axis=-1, keepdims=True)
        heads.append(_dot(pm.astype(BF16), mkv[:, MEM_WIDTH + hd * MEM_HEAD_DIM:
                                                MEM_WIDTH + (hd + 1) * MEM_HEAD_DIM]))
    yc = jnp.concatenate(heads, axis=-1)
    merged = merged + branch(yc, 2, wc_ref, 2)

    out = _dot(merged.astype(BF16), wo_ref[...])
    ms = jnp.mean(out * out, axis=-1, keepdims=True)
    o_ref[...] = x_ref[...] + out * lax.rsqrt(ms + RMS_EPS) * pg_ref[...]


def _final(x2d, h, ya, yb, mkv, wz, wg, wqc, bg, wa, wb, wc, wo, pg, seq, tm=256):
    n, d = x2d.shape
    per_b = seq // tm

    def const(a):
        nd = a.ndim
        return pl.BlockSpec(a.shape, lambda i: (0,) * nd, pipeline_mode=pl.Buffered(1))

    def rows(width):
        return pl.BlockSpec((tm, width), lambda i: (i, 0))

    return pl.pallas_call(
        _final_kernel,
        out_shape=jax.ShapeDtypeStruct((n, d), F32),
        grid=(n // tm,),
        in_specs=[rows(d), rows(d), rows(AB_WIDTH), rows(AB_WIDTH),
                  pl.BlockSpec((1,) + mkv.shape[1:], lambda i: (i // per_b, 0, 0)),
                  const(wz), const(wg), const(wqc), const(bg),
                  const(wa), const(wb), const(wc), const(wo), const(pg)],
        out_specs=rows(d),
        compiler_params=_params(("parallel",)),
        name="final",
    )(x2d, h, ya, yb, mkv, wz, wg, wqc, bg, wa, wb, wc, wo, pg)


def _rotate_half_cols(w):
    d, c = w.shape
    w4 = w.reshape(d, c // HEAD_DIM, 2, HALF)
    return jnp.stack([-w4[:, :, 1], w4[:, :, 0]], axis=2).reshape(d, c)


def kernel(x, mem, positions, pre_g, post_g, mem_g, w_in, b_gate, w_mem_kv,
           w_br_a, w_br_b, w_br_c, w_out):
    batch, seq, d = x.shape
    n = batch * seq
    depth = w_in.shape[0]
    topk = min(MAX_TOPK, seq // 4)
    aw = AB_WIDTH
    sizes = (aw, aw, aw, aw, aw, aw, aw, aw, IDX_HEADS * IDX_DIM, IDX_DIM, IDX_HEADS,
             MEM_WIDTH, MEM_WIDTH, N_BRANCHES * D_MODEL)
    offs = np.concatenate([[0], np.cumsum(sizes)])

    inv = (ROPE_THETA ** (-np.arange(HALF, dtype=np.float32) / HALF)).astype(np.float32)
    pos_rep = jnp.repeat(positions.reshape(n // 4, 4), HALF, axis=1)
    cos_c, sin_c = _rope_tables_nat(pos_rep, jnp.asarray(np.tile(inv, 4)).reshape(1, LANES))
    cos_n = jnp.tile(cos_c.reshape(n, HALF), (1, LANES // HALF))
    sin_n = jnp.tile(sin_c.reshape(n, HALF), (1, LANES // HALF))
    cos_t, sin_t = _rope_tables_tr(positions.reshape(batch, 1, seq), jnp.asarray(inv).reshape(HALF, 1))

    x2d = x.reshape(n, d)
    for layer in range(depth):
        w = w_in[layer]
        col = lambda k: w[:, offs[k]:offs[k + 1]]
        (w_qa, w_ka, w_va, w_za, w_qb, w_kb, w_vb, w_zb, w_qi, w_ki, w_wi, w_qc, w_zc, w_gt) = (
            col(k) for k in range(14))
        scale = HEAD_DIM ** -0.5
        w_nat = jnp.concatenate([w_qa * scale, w_va], axis=1).astype(BF16)
        w_kn = jnp.concatenate([w_kb, w_ki, jnp.zeros((d, LANES - IDX_DIM), F32)], axis=1)
        w_kn_rot = _rotate_half_cols(w_kn).astype(BF16)
        w_kn = w_kn.astype(BF16)
        w_tr = jnp.concatenate([w_ka, w_vb], axis=1).T.astype(BF16)
        w_qt = jnp.concatenate([w_qb * scale, w_qi * (IDX_DIM ** -0.5), w_wi * (IDX_HEADS ** -0.5),
                                jnp.zeros((d, 8 - IDX_HEADS), F32)], axis=1).T.astype(BF16)
        w_z = jnp.concatenate([w_za, w_zb, w_zc], axis=1).astype(BF16)

        h = _rmsnorm_bf16(x2d, pre_g[layer])
        nat = _proj_nat(h, w_nat)
        kn = _proj_nat_rope(h, w_kn, w_kn_rot, cos_n, sin_n)
        tr = _proj_tr(h, w_tr, batch, seq)
        q_t, wi_t = _proj_tr_rope(h, w_qt, cos_t, sin_t, batch, seq,
                                  (aw + IDX_HEADS * IDX_DIM) // HEAD_DIM)
        mkv = _memkv(mem, mem_g[layer], w_mem_kv[layer].astype(BF16))

        ya = _stick_breaking(nat, tr, batch, seq)
        yb = _dsa(kn, tr, q_t, wi_t, batch, seq, topk)
        x2d = _final(x2d, h, ya, yb, mkv, w_z, w_gt.astype(BF16), w_qc.astype(BF16),
                     b_gate[layer].reshape(1, -1), w_br_a[layer].astype(BF16),
                     w_br_b[layer].astype(BF16), w_br_c[layer].astype(BF16),
                     w_out[layer].astype(BF16), post_g[layer].reshape(1, d), seq)
    return x2d.reshape(batch, seq, d)
```

```python
import functools

import numpy as np
import jax
import jax.numpy as jnp
from jax import lax
from jax.experimental import pallas as pl
from jax.experimental.pallas import tpu as pltpu

F32 = jnp.float32
BF16 = jnp.bfloat16

D_MODEL = 1024
CHUNK = 64
ROPE_THETA = 10000.0
RMS_EPS = 1e-6
HEAD_DIM = 64
HALF = HEAD_DIM // 2
AB_WIDTH = 512
IDX_HEADS = 4
IDX_DIM = 64
MAX_TOPK = 256
MEM_HEADS = 4
MEM_HEAD_DIM = 128
MEM_WIDTH = MEM_HEADS * MEM_HEAD_DIM
N_BRANCHES = 3

LANES = 128
T_BLK = 256
NEG_BIG = -1e30
VMEM_LIMIT = 56 * 1024 * 1024

_NT_DIMS = (((1,), (1,)), ((), ()))


def _dot(a, b):
    return jnp.dot(a, b, preferred_element_type=F32)


def _params(sem, vmem=VMEM_LIMIT):
    return pltpu.CompilerParams(dimension_semantics=sem, vmem_limit_bytes=vmem)


def _rms_kernel(x_ref, g_ref, o_ref):
    x = x_ref[...]
    ms = jnp.mean(x * x, axis=-1, keepdims=True)
    o_ref[...] = (x * lax.rsqrt(ms + RMS_EPS) * g_ref[...]).astype(o_ref.dtype)


def _rmsnorm_bf16(x2d, g, tm=512):
    n, d = x2d.shape
    return pl.pallas_call(
        _rms_kernel,
        out_shape=jax.ShapeDtypeStruct((n, d), BF16),
        grid=(n // tm,),
        in_specs=[pl.BlockSpec((tm, d), lambda i: (i, 0)),
                  pl.BlockSpec((1, d), lambda i: (0, 0))],
        out_specs=pl.BlockSpec((tm, d), lambda i: (i, 0)),
        compiler_params=_params(("parallel",)),
        name="rmsnorm",
    )(x2d, g.reshape(1, d))


def _rope_table_kernel(pos_ref, inv_ref, cos_ref, sin_ref):
    ang = pos_ref[...].astype(F32) * inv_ref[...]
    cos_ref[...] = jnp.cos(ang)
    sin_ref[...] = jnp.sin(ang)


def _rope_tables_nat(pos_rep, inv_row):
    r = pos_rep.shape[0]
    br = min(512, r)
    out = jax.ShapeDtypeStruct((r, LANES), F32)
    return pl.pallas_call(
        _rope_table_kernel,
        out_shape=(out, out),
        grid=(r // br,),
        in_specs=[pl.BlockSpec((br, LANES), lambda i: (i, 0)),
                  pl.BlockSpec((1, LANES), lambda i: (0, 0))],
        out_specs=(pl.BlockSpec((br, LANES), lambda i: (i, 0)),
                   pl.BlockSpec((br, LANES), lambda i: (i, 0))),
        compiler_params=_params(("parallel",)),
        name="rope_tables_nat",
    )(pos_rep, inv_row)


def _rope_table_t_kernel(pos_ref, inv_ref, cos_ref, sin_ref):
    ang = pos_ref[0].astype(F32) * inv_ref[...]
    cos_ref[0] = jnp.cos(ang)
    sin_ref[0] = jnp.sin(ang)


def _rope_tables_tr(pos3, inv_col):
    b, _, s = pos3.shape
    ts = min(512, s)
    out = jax.ShapeDtypeStruct((b, HALF, s), F32)
    return pl.pallas_call(
        _rope_table_t_kernel,
        out_shape=(out, out),
        grid=(b, s // ts),
        in_specs=[pl.BlockSpec((1, 1, ts), lambda i, j: (i, 0, j)),
                  pl.BlockSpec((HALF, 1), lambda i, j: (0, 0))],
        out_specs=(pl.BlockSpec((1, HALF, ts), lambda i, j: (i, 0, j)),
                   pl.BlockSpec((1, HALF, ts), lambda i, j: (i, 0, j))),
        compiler_params=_params(("parallel", "parallel")),
        name="rope_tables_tr",
    )(pos3, inv_col)


def _proj_nat_kernel(h_ref, w_ref, o_ref):
    o_ref[...] = _dot(h_ref[...], w_ref[...]).astype(o_ref.dtype)


def _proj_nat(h, w, tm=512):
    n, d = h.shape
    c = w.shape[1]
    return pl.pallas_call(
        _proj_nat_kernel,
        out_shape=jax.ShapeDtypeStruct((n, c), BF16),
        grid=(n // tm,),
        in_specs=[pl.BlockSpec((tm, d), lambda i: (i, 0)),
                  pl.BlockSpec((d, c), lambda i: (0, 0))],
        out_specs=pl.BlockSpec((tm, c), lambda i: (i, 0)),
        compiler_params=_params(("parallel",)),
        name="proj_nat",
    )(h, w)


def _proj_nat_rope_kernel(h_ref, w_ref, wr_ref, cos_ref, sin_ref, o_ref):
    h = h_ref[...]
    a = _dot(h, w_ref[...])
    b = _dot(h, wr_ref[...])
    c = cos_ref[...]
    s = sin_ref[...]
    for g in range(o_ref.shape[1] // LANES):
        sl = slice(g * LANES, (g + 1) * LANES)
        o_ref[:, sl] = (a[:, sl] * c + b[:, sl] * s).astype(o_ref.dtype)


def _proj_nat_rope(h, w, wr, cos_n, sin_n, tm=512):
    n, d = h.shape
    c = w.shape[1]
    return pl.pallas_call(
        _proj_nat_rope_kernel,
        out_shape=jax.ShapeDtypeStruct((n, c), BF16),
        grid=(n // tm,),
        in_specs=[pl.BlockSpec((tm, d), lambda i: (i, 0)),
                  pl.BlockSpec((d, c), lambda i: (0, 0)),
                  pl.BlockSpec((d, c), lambda i: (0, 0)),
                  pl.BlockSpec((tm, LANES), lambda i: (i, 0)),
                  pl.BlockSpec((tm, LANES), lambda i: (i, 0))],
        out_specs=pl.BlockSpec((tm, c), lambda i: (i, 0)),
        compiler_params=_params(("parallel",)),
        name="proj_nat_rope",
    )(h, w, wr, cos_n, sin_n)


def _proj_tr_kernel(wt_ref, h_ref, o_ref):
    r = lax.dot_general(wt_ref[...], h_ref[...], _NT_DIMS, preferred_element_type=F32)
    o_ref[0, 0] = r.astype(o_ref.dtype)


def _proj_tr(h, wt, batch, seq):
    c, d = wt.shape
    nblk = seq // T_BLK
    return pl.pallas_call(
        _proj_tr_kernel,
        out_shape=jax.ShapeDtypeStruct((batch, nblk, c, T_BLK), BF16),
        grid=(batch, nblk),
        in_specs=[pl.BlockSpec((c, d), lambda b, i: (0, 0)),
                  pl.BlockSpec((T_BLK, d), lambda b, i: (b * nblk + i, 0))],
        out_specs=pl.BlockSpec((1, 1, c, T_BLK), lambda b, i: (b, i, 0, 0)),
        compiler_params=_params(("parallel", "parallel")),
        name="proj_tr",
    )(wt, h)


def _proj_tr_rope_kernel(wt_ref, h_ref, cos_ref, sin_ref, o_ref, w_ref, *, n_rope_heads):
    r = lax.dot_general(wt_ref[...], h_ref[...], _NT_DIMS, preferred_element_type=F32)
    c = cos_ref[0]
    s = sin_ref[0]
    for hd in range(n_rope_heads):
        lo = hd * HEAD_DIM
        t1 = r[lo:lo + HALF]
        t2 = r[lo + HALF:lo + HEAD_DIM]
        o_ref[0, 0, lo:lo + HALF, :] = (t1 * c - t2 * s).astype(o_ref.dtype)
        o_ref[0, 0, lo + HALF:lo + HEAD_DIM, :] = (t2 * c + t1 * s).astype(o_ref.dtype)
    w_ref[0, 0] = r[n_rope_heads * HEAD_DIM:]


def _proj_tr_rope(h, wt, cos_t, sin_t, batch, seq, n_rope_heads):
    c, d = wt.shape
    c_rope = n_rope_heads * HEAD_DIM
    c_rest = c - c_rope
    nblk = seq // T_BLK
    return pl.pallas_call(
        functools.partial(_proj_tr_rope_kernel, n_rope_heads=n_rope_heads),
        out_shape=(jax.ShapeDtypeStruct((batch, nblk, c_rope, T_BLK), BF16),
                   jax.ShapeDtypeStruct((batch, nblk, c_rest, T_BLK), F32)),
        grid=(batch, nblk),
        in_specs=[pl.BlockSpec((c, d), lambda b, i: (0, 0)),
                  pl.BlockSpec((T_BLK, d), lambda b, i: (b * nblk + i, 0)),
                  pl.BlockSpec((1, HALF, T_BLK), lambda b, i: (b, 0, i)),
                  pl.BlockSpec((1, HALF, T_BLK), lambda b, i: (b, 0, i))],
        out_specs=(pl.BlockSpec((1, 1, c_rope, T_BLK), lambda b, i: (b, i, 0, 0)),
                   pl.BlockSpec((1, 1, c_rest, T_BLK), lambda b, i: (b, i, 0, 0))),
        compiler_params=_params(("parallel", "parallel")),
        name="proj_tr_rope",
    )(wt, h, cos_t, sin_t)


def _memkv_kernel(mem_ref, g_ref, w_ref, o_ref):
    x = mem_ref[0]
    ms = jnp.mean(x * x, axis=-1, keepdims=True)
    hn = (x * lax.rsqrt(ms + RMS_EPS) * g_ref[...]).astype(BF16)
    o_ref[0] = _dot(hn, w_ref[...]).astype(o_ref.dtype)


def _memkv(mem, g, w):
    b, m, d = mem.shape
    c = w.shape[1]
    return pl.pallas_call(
        _memkv_kernel,
        out_shape=jax.ShapeDtypeStruct((b, m, c), BF16),
        grid=(b,),
        in_specs=[pl.BlockSpec((1, m, d), lambda i: (i, 0, 0)),
                  pl.BlockSpec((1, d), lambda i: (0, 0)),
                  pl.BlockSpec((d, c), lambda i: (0, 0))],
        out_specs=pl.BlockSpec((1, m, c), lambda i: (i, 0, 0)),
        compiler_params=_params(("parallel",)),
        name="mem_kv",
    )(mem, g.reshape(1, d), w)


STICK_DEAD_LOG = -110.0
STICK_PAIRS = 2


def _stick_kernel(q_ref, kT_ref, v_ref, o_ref):
    t = T_BLK
    i = pl.program_id(2)
    lane = lax.broadcasted_iota(jnp.int32, (t, LANES), 1)
    zero_b = jnp.zeros((), BF16)
    lo_half, hi_half = lane < HEAD_DIM, lane >= HEAD_DIM

    def pair_cols(p):
        return slice(p * LANES, (p + 1) * LANES)

    q_heads = []
    for p in range(STICK_PAIRS):
        qp = q_ref[:, pair_cols(p)]
        q_heads += [jnp.where(lo_half, qp, zero_b), jnp.where(hi_half, qp, zero_b)]
    row = lax.broadcasted_iota(jnp.int32, (t, t), 0)
    col = lax.broadcasted_iota(jnp.int32, (t, t), 1)
    tri = jnp.where(row > col, 1.0, 0.0).astype(BF16)
    causal = col < row

    def block(j, accs, rs, diag):
        accs, rs = list(accs), list(rs)
        for p in range(STICK_PAIRS):
            k_t = kT_ref[0, j, pair_cols(p), :]
            v = v_ref[pl.ds(pl.multiple_of(j * t, t), t), pair_cols(p)]
            v_heads = (jnp.where(lo_half, v, zero_b), jnp.where(hi_half, v, zero_b))
            for hh in range(2):
                h = 2 * p + hh
                z = _dot(q_heads[h], k_t)
                sp = jnp.log(1.0 + jnp.exp(-jnp.abs(z)))
                lb = jnp.minimum(z, 0.0) - sp
                l1 = lb - z
                if diag:
                    l1 = jnp.where(causal, l1, 0.0)
                hi_f = lax.bitcast_convert_type(
                    lax.bitcast_convert_type(l1, jnp.int32) & jnp.int32(-65536), F32)
                cum = _dot(hi_f.astype(BF16), tri) + _dot((l1 - hi_f).astype(BF16), tri)
                a = jnp.exp(lb + cum + rs[h])
                if diag:
                    a = jnp.where(causal, a, 0.0)
                accs[p] = accs[p] + _dot(a.astype(BF16), v_heads[hh])
                rs[h] = rs[h] + jnp.sum(l1, axis=1, keepdims=True)
        return tuple(accs), tuple(rs)

    def r_max(rs):
        m = rs[0]
        for r in rs[1:]:
            m = jnp.maximum(m, r)
        return jnp.max(m)

    accs = tuple(jnp.zeros((t, LANES), F32) for _ in range(STICK_PAIRS))
    rs = tuple(jnp.zeros((t, 1), F32) for _ in range(2 * STICK_PAIRS))
    accs, rs = block(i, accs, rs, True)

    def cond(c):
        return jnp.logical_and(c[0] >= 0, c[1] > STICK_DEAD_LOG)

    def body(c):
        j, _, accs, rs = c
        accs, rs = block(j, accs, rs, False)
        return j - 1, r_max(rs), accs, rs

    _, _, accs, _ = lax.while_loop(cond, body, (i - 1, r_max(rs), accs, rs))
    for p in range(STICK_PAIRS):
        o_ref[:, pair_cols(p)] = accs[p]


def _stick_breaking(nat, tr, batch, seq):
    t = T_BLK
    nblk = seq // t
    w = STICK_PAIRS * LANES
    ngroups = AB_WIDTH // w
    n = batch * seq
    return pl.pallas_call(
        _stick_kernel,
        out_shape=jax.ShapeDtypeStruct((n, AB_WIDTH), F32),
        grid=(batch, ngroups, nblk),
        in_specs=[pl.BlockSpec((t, w), lambda b, p, i: (b * nblk + i, p)),
                  pl.BlockSpec((1, nblk, w, t), lambda b, p, i: (b, 0, p, 0)),
                  pl.BlockSpec((seq, w), lambda b, p, i: (b, ngroups + p))],
        out_specs=pl.BlockSpec((t, w), lambda b, p, i: (b * nblk + i, p)),
        compiler_params=_params(("parallel", "parallel", "arbitrary")),
        name="stick_breaking",
    )(nat, tr, nat)


COUNT_ROWS = 32
DSA_PAIRS_PER_LOOP = 4


def _key_to_float(key):
    bits = key ^ ((key >> 31) & jnp.int32(0x7FFFFFFF))
    return lax.bitcast_convert_type(bits, F32)


def _dsa_kernel(kn_ref, vT_ref, qT_ref, wi_ref, o_ref, sc_ref, bias_ref, *, topk):
    t = T_BLK
    i = pl.program_id(1)
    nblk = i + 1
    q_all = qT_ref[0, 0]
    wi = wi_ref[0, 0]
    tq = i * t + lax.broadcasted_iota(jnp.int32, (1, t), 1)
    limit = ((tq >> 6) + 1) << 6
    kcount = jnp.minimum(limit, topk).astype(F32)
    krow = lax.broadcasted_iota(jnp.int32, (t, t), 0)
    kcol = lax.broadcasted_iota(jnp.int32, (t, t), 1)
    qi_off = AB_WIDTH

    def rows(j):
        return pl.ds(pl.multiple_of(j * t, t), t)

    def score_body(j, _):
        ki = kn_ref[rows(j), AB_WIDTH:AB_WIDTH + IDX_DIM]
        sc = jnp.zeros((t, t), F32)
        for h in range(IDX_HEADS):
            qh = q_all[qi_off + h * IDX_DIM:qi_off + (h + 1) * IDX_DIM]
            sc = sc + wi[h:h + 1] * jnp.maximum(_dot(ki, qh), 0.0)
        sc_ref[rows(j), :] = jnp.where(j * t + krow < limit, sc, -jnp.inf)
        return 0

    lax.fori_loop(0, nblk, score_body, 0)

    @pl.when(nblk % 2 == 1)
    def _():
        sc_ref[rows(nblk), :] = jnp.full((t, t), -jnp.inf, F32)

    def count(pred):
        def body(j, c):
            s = sc_ref[pl.ds(pl.multiple_of(j * 2 * t, 2 * t), 2 * t), :]
            ind = jnp.where(pred(s), 1.0, 0.0)
            return c + jnp.sum(ind.reshape(2 * t // COUNT_ROWS, COUNT_ROWS, t), axis=0)
        c = lax.fori_loop(0, (nblk + 1) // 2, body, jnp.zeros((COUNT_ROWS, t), F32))
        return jnp.sum(c, axis=0, keepdims=True)

    def search_body(it, p):
        cand = p + lax.shift_left(jnp.int32(1), 31 - it)
        cf = _key_to_float(cand)
        cnt = count(lambda s: s >= cf)
        return jnp.where(cnt >= kcount, cand, p)

    p = lax.fori_loop(0, 32, search_body, jnp.full((1, t), jnp.iinfo(jnp.int32).min, jnp.int32))
    thr = _key_to_float(p)
    n_ties = kcount - count(lambda s: s > thr)

    tri = jnp.where(kcol < krow, 1.0, 0.0).astype(BF16)

    def bias_body(j, seen):
        s = sc_ref[rows(j), :]
        tie = s == thr
        tie_f = jnp.where(tie, 1.0, 0.0)
        before = _dot(tri, tie_f.astype(BF16)) + seen
        keep_tie = jnp.where(before < n_ties, 0.0, NEG_BIG)
        bias_ref[rows(j), :] = jnp.where(s > thr, 0.0, jnp.where(tie, keep_tie, NEG_BIG))
        return seen + jnp.sum(tie_f, axis=0, keepdims=True)

    lax.fori_loop(0, nblk, bias_body, jnp.zeros((1, t), F32))

    lane = lax.broadcasted_iota(jnp.int32, (t, LANES), 1)
    prow = lax.broadcasted_iota(jnp.int32, (LANES, t), 0)
    zero_b = jnp.zeros((), BF16)
    npairs = AB_WIDTH // LANES
    sel_k = (lane < HEAD_DIM, lane >= HEAD_DIM)
    sel_v = (prow < HEAD_DIM, prow >= HEAD_DIM)

    for g0 in range(0, npairs, DSA_PAIRS_PER_LOOP):
        pairs = range(g0, g0 + DSA_PAIRS_PER_LOOP)

        def logits(j, pairs=pairs):
            out = []
            for pr in pairs:
                cols = slice(pr * LANES, (pr + 1) * LANES)
                kb = kn_ref[rows(j), cols]
                for hh in range(2):
                    out.append(_dot(jnp.where(sel_k[hh], kb, zero_b), q_all[cols]))
            return tuple(out)

        def att_body(j, carry, pairs=pairs, logits=logits):
            ms, ls, accs, s_cur = (list(c) for c in carry)
            s_next = logits(jnp.minimum(j + 1, nblk - 1))
            bias = bias_ref[rows(j), :]
            for ip, pr in enumerate(pairs):
                v_t = vT_ref[0, j, pr * LANES:(pr + 1) * LANES, :]
                alphas, pvs = [], []
                for hh in range(2):
                    h = 2 * ip + hh
                    s = s_cur[h] + bias
                    m_new = jnp.maximum(ms[h], jnp.max(s, axis=0, keepdims=True))
                    alpha = jnp.exp(ms[h] - m_new)
                    pm = jnp.exp(s - m_new)
                    ls[h] = alpha * ls[h] + jnp.sum(pm, axis=0, keepdims=True)
                    ms[h] = m_new
                    alphas.append(alpha)
                    pvs.append(_dot(jnp.where(sel_v[hh], v_t, zero_b), pm.astype(BF16)))
                accs[ip] = accs[ip] * jnp.where(sel_v[0], alphas[0], alphas[1]) + pvs[0] + pvs[1]
            return tuple(ms), tuple(ls), tuple(accs), s_next

        nh = 2 * DSA_PAIRS_PER_LOOP
        init = (tuple(jnp.full((1, t), NEG_BIG, F32) for _ in range(nh)),
                tuple(jnp.zeros((1, t), F32) for _ in range(nh)),
                tuple(jnp.zeros((LANES, t), F32) for _ in range(DSA_PAIRS_PER_LOOP)),
                logits(0))
        _, ls, accs, _ = lax.fori_loop(0, nblk, att_body, init)
        for ip, pr in enumerate(pairs):
            out_t = accs[ip] / jnp.where(sel_v[0], ls[2 * ip], ls[2 * ip + 1])
            o_ref[:, pr * LANES:(pr + 1) * LANES] = out_t.T


def _dsa(kn, tr, q_t, wi_t, batch, seq, topk):
    t = T_BLK
    nblk = seq // t
    n = batch * seq
    ckn = kn.shape[1]
    cq = q_t.shape[2]
    return pl.pallas_call(
        functools.partial(_dsa_kernel, topk=topk),
        out_shape=jax.ShapeDtypeStruct((n, AB_WIDTH), F32),
        grid=(batch, nblk),
        in_specs=[pl.BlockSpec((seq, ckn), lambda b, i: (b, 0)),
                  pl.BlockSpec((1, nblk, AB_WIDTH, t), lambda b, i: (b, 0, 1, 0)),
                  pl.BlockSpec((1, 1, cq, t), lambda b, i: (b, i, 0, 0)),
                  pl.BlockSpec((1, 1, 8, t), lambda b, i: (b, i, 0, 0))],
        out_specs=pl.BlockSpec((t, AB_WIDTH), lambda b, i: (b * nblk + i, 0)),
        scratch_shapes=[pltpu.VMEM((seq + t, t), F32), pltpu.VMEM((seq, t), F32)],
        compiler_params=_params(("parallel", "arbitrary")),
        name="dsa",
    )(kn, tr, q_t, wi_t)


def _sigmoid(x):
    return 1.0 / (1.0 + jnp.exp(-x))


def _final_kernel(x_ref, h_ref, ya_ref, yb_ref, mkv_ref, wz_ref, wg_ref, wqc_ref, bg_ref,
                  wa_ref, wb_ref, wc_ref, wo_ref, pg_ref, o_ref):
    h = h_ref[...]
    w = AB_WIDTH

    def branch(y, zi, w_ref, gi):
        z = _dot(h, wz_ref[:, zi * w:(zi + 1) * w])
        u = (y * (z * _sigmoid(z))).astype(BF16)
        g = _sigmoid(_dot(h, wg_ref[:, gi * D_MODEL:(gi + 1) * D_MODEL])
                     + bg_ref[:, gi * D_MODEL:(gi + 1) * D_MODEL])
        return g * _dot(u, w_ref[...])

    merged = branch(ya_ref[...], 0, wa_ref, 0)
    merged = merged + branch(yb_ref[...], 1, wb_ref, 1)

    qc = _dot(h, wqc_ref[...]).astype(BF16)
    mkv = mkv_ref[0]
    scale = MEM_HEAD_DIM ** -0.5
    heads = []
    for hd in range(MEM_HEADS):
        sl = slice(hd * MEM_HEAD_DIM, (hd + 1) * MEM_HEAD_DIM)
        logits = lax.dot_general(qc[:, sl], mkv[:, sl], _NT_DIMS,
                                 preferred_element_type=F32) * scale
        mx = jnp.max(logits, axis=-1, keepdims=True)
        e = jnp.exp(logits - mx)
        pm = e / jnp.sum(e, axis=-1, keepdims=True)
        heads.append(_dot(pm.astype(BF16), mkv[:, MEM_WIDTH + hd * MEM_HEAD_DIM:
                                                MEM_WIDTH + (hd + 1) * MEM_HEAD_DIM]))
    yc = jnp.concatenate(heads, axis=-1)
    merged = merged + branch(yc, 2, wc_ref, 2)

    out = _dot(merged.astype(BF16), wo_ref[...])
    ms = jnp.mean(out * out, axis=-1, keepdims=True)
    o_ref[...] = x_ref[...] + out * lax.rsqrt(ms + RMS_EPS) * pg_ref[...]


def _final(x2d, h, ya, yb, mkv, wz, wg, wqc, bg, wa, wb, wc, wo, pg, seq, tm=256):
    n, d = x2d.shape
    per_b = seq // tm

    def const(a):
        nd = a.ndim
        return pl.BlockSpec(a.shape, lambda i: (0,) * nd, pipeline_mode=pl.Buffered(1))

    def rows(width):
        return pl.BlockSpec((tm, width), lambda i: (i, 0))

    return pl.pallas_call(
        _final_kernel,
        out_shape=jax.ShapeDtypeStruct((n, d), F32),
        grid=(n // tm,),
        in_specs=[rows(d), rows(d), rows(AB_WIDTH), rows(AB_WIDTH),
                  pl.BlockSpec((1,) + mkv.shape[1:], lambda i: (i // per_b, 0, 0)),
                  const(wz), const(wg), const(wqc), const(bg),
                  const(wa), const(wb), const(wc), const(wo), const(pg)],
        out_specs=rows(d),
        compiler_params=_params(("parallel",)),
        name="final",
    )(x2d, h, ya, yb, mkv, wz, wg, wqc, bg, wa, wb, wc, wo, pg)


def _rotate_half_cols(w):
    d, c = w.shape
    w4 = w.reshape(d, c // HEAD_DIM, 2, HALF)
    return jnp.stack([-w4[:, :, 1], w4[:, :, 0]], axis=2).reshape(d, c)


def kernel(x, mem, positions, pre_g, post_g, mem_g, w_in, b_gate, w_mem_kv,
           w_br_a, w_br_b, w_br_c, w_out):
    batch, seq, d = x.shape
    n = batch * seq
    depth = w_in.shape[0]
    topk = min(MAX_TOPK, seq // 4)
    aw = AB_WIDTH
    sizes = (aw, aw, aw, aw, aw, aw, aw, aw, IDX_HEADS * IDX_DIM, IDX_DIM, IDX_HEADS,
             MEM_WIDTH, MEM_WIDTH, N_BRANCHES * D_MODEL)
    offs = np.concatenate([[0], np.cumsum(sizes)])

    inv = (ROPE_THETA ** (-np.arange(HALF, dtype=np.float32) / HALF)).astype(np.float32)
    pos_rep = jnp.repeat(positions.reshape(n // 4, 4), HALF, axis=1)
    cos_c, sin_c = _rope_tables_nat(pos_rep, jnp.asarray(np.tile(inv, 4)).reshape(1, LANES))
    cos_n = jnp.tile(cos_c.reshape(n, HALF), (1, LANES // HALF))
    sin_n = jnp.tile(sin_c.reshape(n, HALF), (1, LANES // HALF))
    cos_t, sin_t = _rope_tables_tr(positions.reshape(batch, 1, seq), jnp.asarray(inv).reshape(HALF, 1))

    x2d = x.reshape(n, d)
    for layer in range(depth):
        w = w_in[layer]
        col = lambda k: w[:, offs[k]:offs[k + 1]]
        (w_qa, w_ka, w_va, w_za, w_qb, w_kb, w_vb, w_zb, w_qi, w_ki, w_wi, w_qc, w_zc, w_gt) = (
            col(k) for k in range(14))
        scale = HEAD_DIM ** -0.5
        w_nat = jnp.concatenate([w_qa * scale, w_va], axis=1).astype(BF16)
        w_kn = jnp.concatenate([w_kb, w_ki, jnp.zeros((d, LANES - IDX_DIM), F32)], axis=1)
        w_kn_rot = _rotate_half_cols(w_kn).astype(BF16)
        w_kn = w_kn.astype(BF16)
        w_tr = jnp.concatenate([w_ka, w_vb], axis=1).T.astype(BF16)
        w_qt = jnp.concatenate([w_qb * scale, w_qi * (IDX_DIM ** -0.5), w_wi * (IDX_HEADS ** -0.5),
                                jnp.zeros((d, 8 - IDX_HEADS), F32)], axis=1).T.astype(BF16)
        w_z = jnp.concatenate([w_za, w_zb, w_zc], axis=1).astype(BF16)

        h = _rmsnorm_bf16(x2d, pre_g[layer])
        nat = _proj_nat(h, w_nat)
        kn = _proj_nat_rope(h, w_kn, w_kn_rot, cos_n, sin_n)
        tr = _proj_tr(h, w_tr, batch, seq)
        q_t, wi_t = _proj_tr_rope(h, w_qt, cos_t, sin_t, batch, seq,
                                  (aw + IDX_HEADS * IDX_DIM) // HEAD_DIM)
        mkv = _memkv(mem, mem_g[layer], w_mem_kv[layer].astype(BF16))

        ya = _stick_breaking(nat, tr, batch, seq)
        yb = _dsa(kn, tr, q_t, wi_t, batch, seq, topk)
        x2d = _final(x2d, h, ya, yb, mkv, w_z, w_gt.astype(BF16), w_qc.astype(BF16),
                     b_gate[layer].reshape(1, -1), w_br_a[layer].astype(BF16),
                     w_br_b[layer].astype(BF16), w_br_c[layer].astype(BF16),
                     w_out[layer].astype(BF16), post_g[layer].reshape(1, d), seq)
    return x2d.reshape(batch, seq, d)
```

```python
import functools

import numpy as np
import jax
import jax.numpy as jnp
from jax import lax
from jax.experimental import pallas as pl
from jax.experimental.pallas import tpu as pltpu

F32 = jnp.float32
BF16 = jnp.bfloat16

D_MODEL = 1024
CHUNK = 64
ROPE_THETA = 10000.0
RMS_EPS = 1e-6
HEAD_DIM = 64
HALF = HEAD_DIM // 2
AB_WIDTH = 512
IDX_HEADS = 4
IDX_DIM = 64
MAX_TOPK = 256
MEM_HEADS = 4
MEM_HEAD_DIM = 128
MEM_WIDTH = MEM_HEADS * MEM_HEAD_DIM
N_BRANCHES = 3

LANES = 128
T_BLK = 256
NEG_BIG = -1e30
LOG2_E = 1.4426950408889634
VMEM_LIMIT = 56 * 1024 * 1024

_NT_DIMS = (((1,), (1,)), ((), ()))


def _dot(a, b):
    return jnp.dot(a, b, preferred_element_type=F32)


def _params(sem, vmem=VMEM_LIMIT):
    return pltpu.CompilerParams(dimension_semantics=sem, vmem_limit_bytes=vmem)


def _rms_kernel(x_ref, g_ref, o_ref):
    x = x_ref[...]
    ms = jnp.mean(x * x, axis=-1, keepdims=True)
    o_ref[...] = (x * lax.rsqrt(ms + RMS_EPS) * g_ref[...]).astype(o_ref.dtype)


def _rmsnorm_bf16(x2d, g, tm=512):
    n, d = x2d.shape
    return pl.pallas_call(
        _rms_kernel,
        out_shape=jax.ShapeDtypeStruct((n, d), BF16),
        grid=(n // tm,),
        in_specs=[pl.BlockSpec((tm, d), lambda i: (i, 0)),
                  pl.BlockSpec((1, d), lambda i: (0, 0))],
        out_specs=pl.BlockSpec((tm, d), lambda i: (i, 0)),
        compiler_params=_params(("parallel",)),
        name="rmsnorm",
    )(x2d, g.reshape(1, d))


def _rope_table_kernel(pos_ref, inv_ref, cos_ref, sin_ref):
    ang = pos_ref[...].astype(F32) * inv_ref[...]
    cos_ref[...] = jnp.cos(ang)
    sin_ref[...] = jnp.sin(ang)


def _rope_tables_nat(pos_rep, inv_row):
    r = pos_rep.shape[0]
    br = min(512, r)
    out = jax.ShapeDtypeStruct((r, LANES), F32)
    return pl.pallas_call(
        _rope_table_kernel,
        out_shape=(out, out),
        grid=(r // br,),
        in_specs=[pl.BlockSpec((br, LANES), lambda i: (i, 0)),
                  pl.BlockSpec((1, LANES), lambda i: (0, 0))],
        out_specs=(pl.BlockSpec((br, LANES), lambda i: (i, 0)),
                   pl.BlockSpec((br, LANES), lambda i: (i, 0))),
        compiler_params=_params(("parallel",)),
        name="rope_tables_nat",
    )(pos_rep, inv_row)


def _rope_table_t_kernel(pos_ref, inv_ref, cos_ref, sin_ref):
    ang = pos_ref[0].astype(F32) * inv_ref[...]
    cos_ref[0] = jnp.cos(ang)
    sin_ref[0] = jnp.sin(ang)


def _rope_tables_tr(pos3, inv_col):
    b, _, s = pos3.shape
    ts = min(512, s)
    out = jax.ShapeDtypeStruct((b, HALF, s), F32)
    return pl.pallas_call(
        _rope_table_t_kernel,
        out_shape=(out, out),
        grid=(b, s // ts),
        in_specs=[pl.BlockSpec((1, 1, ts), lambda i, j: (i, 0, j)),
                  pl.BlockSpec((HALF, 1), lambda i, j: (0, 0))],
        out_specs=(pl.BlockSpec((1, HALF, ts), lambda i, j: (i, 0, j)),
                   pl.BlockSpec((1, HALF, ts), lambda i, j: (i, 0, j))),
        compiler_params=_params(("parallel", "parallel")),
        name="rope_tables_tr",
    )(pos3, inv_col)


def _proj_nat_kernel(h_ref, w_ref, o_ref):
    o_ref[...] = _dot(h_ref[...], w_ref[...]).astype(o_ref.dtype)


def _proj_nat(h, w, tm=512):
    n, d = h.shape
    c = w.shape[1]
    return pl.pallas_call(
        _proj_nat_kernel,
        out_shape=jax.ShapeDtypeStruct((n, c), BF16),
        grid=(n // tm,),
        in_specs=[pl.BlockSpec((tm, d), lambda i: (i, 0)),
                  pl.BlockSpec((d, c), lambda i: (0, 0))],
        out_specs=pl.BlockSpec((tm, c), lambda i: (i, 0)),
        compiler_params=_params(("parallel",)),
        name="proj_nat",
    )(h, w)


def _proj_nat_rope_kernel(h_ref, w_ref, wr_ref, cos_ref, sin_ref, o_ref):
    h = h_ref[...]
    a = _dot(h, w_ref[...])
    b = _dot(h, wr_ref[...])
    c = cos_ref[...]
    s = sin_ref[...]
    for g in range(o_ref.shape[1] // LANES):
        sl = slice(g * LANES, (g + 1) * LANES)
        o_ref[:, sl] = (a[:, sl] * c + b[:, sl] * s).astype(o_ref.dtype)


def _proj_nat_rope(h, w, wr, cos_n, sin_n, tm=512):
    n, d = h.shape
    c = w.shape[1]
    return pl.pallas_call(
        _proj_nat_rope_kernel,
        out_shape=jax.ShapeDtypeStruct((n, c), BF16),
        grid=(n // tm,),
        in_specs=[pl.BlockSpec((tm, d), lambda i: (i, 0)),
                  pl.BlockSpec((d, c), lambda i: (0, 0)),
                  pl.BlockSpec((d, c), lambda i: (0, 0)),
                  pl.BlockSpec((tm, LANES), lambda i: (i, 0)),
                  pl.BlockSpec((tm, LANES), lambda i: (i, 0))],
        out_specs=pl.BlockSpec((tm, c), lambda i: (i, 0)),
        compiler_params=_params(("parallel",)),
        name="proj_nat_rope",
    )(h, w, wr, cos_n, sin_n)


def _proj_tr_kernel(wt_ref, h_ref, o_ref):
    r = lax.dot_general(wt_ref[...], h_ref[...], _NT_DIMS, preferred_element_type=F32)
    o_ref[0, 0] = r.astype(o_ref.dtype)


def _proj_tr(h, wt, batch, seq):
    c, d = wt.shape
    nblk = seq // T_BLK
    return pl.pallas_call(
        _proj_tr_kernel,
        out_shape=jax.ShapeDtypeStruct((batch, nblk, c, T_BLK), BF16),
        grid=(batch, nblk),
        in_specs=[pl.BlockSpec((c, d), lambda b, i: (0, 0)),
                  pl.BlockSpec((T_BLK, d), lambda b, i: (b * nblk + i, 0))],
        out_specs=pl.BlockSpec((1, 1, c, T_BLK), lambda b, i: (b, i, 0, 0)),
        compiler_params=_params(("parallel", "parallel")),
        name="proj_tr",
    )(wt, h)


def _proj_tr_rope_kernel(wt_ref, h_ref, cos_ref, sin_ref, o_ref, w_ref, *, n_rope_heads):
    r = lax.dot_general(wt_ref[...], h_ref[...], _NT_DIMS, preferred_element_type=F32)
    c = cos_ref[0]
    s = sin_ref[0]
    for hd in range(n_rope_heads):
        lo = hd * HEAD_DIM
        t1 = r[lo:lo + HALF]
        t2 = r[lo + HALF:lo + HEAD_DIM]
        o_ref[0, 0, lo:lo + HALF, :] = (t1 * c - t2 * s).astype(o_ref.dtype)
        o_ref[0, 0, lo + HALF:lo + HEAD_DIM, :] = (t2 * c + t1 * s).astype(o_ref.dtype)
    w_ref[0, 0] = r[n_rope_heads * HEAD_DIM:]


def _proj_tr_rope(h, wt, cos_t, sin_t, batch, seq, n_rope_heads):
    c, d = wt.shape
    c_rope = n_rope_heads * HEAD_DIM
    c_rest = c - c_rope
    nblk = seq // T_BLK
    return pl.pallas_call(
        functools.partial(_proj_tr_rope_kernel, n_rope_heads=n_rope_heads),
        out_shape=(jax.ShapeDtypeStruct((batch, nblk, c_rope, T_BLK), BF16),
                   jax.ShapeDtypeStruct((batch, nblk, c_rest, T_BLK), F32)),
        grid=(batch, nblk),
        in_specs=[pl.BlockSpec((c, d), lambda b, i: (0, 0)),
                  pl.BlockSpec((T_BLK, d), lambda b, i: (b * nblk + i, 0)),
                  pl.BlockSpec((1, HALF, T_BLK), lambda b, i: (b, 0, i)),
                  pl.BlockSpec((1, HALF, T_BLK), lambda b, i: (b, 0, i))],
        out_specs=(pl.BlockSpec((1, 1, c_rope, T_BLK), lambda b, i: (b, i, 0, 0)),
                   pl.BlockSpec((1, 1, c_rest, T_BLK), lambda b, i: (b, i, 0, 0))),
        compiler_params=_params(("parallel", "parallel")),
        name="proj_tr_rope",
    )(wt, h, cos_t, sin_t)


def _memkv_kernel(mem_ref, g_ref, w_ref, o_ref):
    x = mem_ref[0]
    ms = jnp.mean(x * x, axis=-1, keepdims=True)
    hn = (x * lax.rsqrt(ms + RMS_EPS) * g_ref[...]).astype(BF16)
    o_ref[0] = _dot(hn, w_ref[...]).astype(o_ref.dtype)


def _memkv(mem, g, w):
    b, m, d = mem.shape
    c = w.shape[1]
    return pl.pallas_call(
        _memkv_kernel,
        out_shape=jax.ShapeDtypeStruct((b, m, c), BF16),
        grid=(b,),
        in_specs=[pl.BlockSpec((1, m, d), lambda i: (i, 0, 0)),
                  pl.BlockSpec((1, d), lambda i: (0, 0)),
                  pl.BlockSpec((d, c), lambda i: (0, 0))],
        out_specs=pl.BlockSpec((1, m, c), lambda i: (i, 0, 0)),
        compiler_params=_params(("parallel",)),
        name="mem_kv",
    )(mem, g.reshape(1, d), w)


STICK_DEAD_LOG = -110.0
STICK_PAIRS = 4


def _stick_kernel(q_ref, kT_ref, v_ref, o_ref):
    t = T_BLK
    i = pl.program_id(2)
    lane = lax.broadcasted_iota(jnp.int32, (t, LANES), 1)
    zero_b = jnp.zeros((), BF16)
    lo_half, hi_half = lane < HEAD_DIM, lane >= HEAD_DIM

    def pair_cols(p):
        return slice(p * LANES, (p + 1) * LANES)

    q_heads = []
    for p in range(STICK_PAIRS):
        qp = q_ref[:, pair_cols(p)]
        q_heads += [jnp.where(lo_half, qp, zero_b), jnp.where(hi_half, qp, zero_b)]
    row = lax.broadcasted_iota(jnp.int32, (t, t), 0)
    col = lax.broadcasted_iota(jnp.int32, (t, t), 1)
    tri = jnp.where(row > col, 1.0, 0.0).astype(BF16)
    causal = col < row

    def block(j, accs, rs, diag):
        accs, rs = list(accs), list(rs)
        heads = range(2 * STICK_PAIRS)
        zs = [_dot(q_heads[h], kT_ref[0, j, pair_cols(h // 2), :]) for h in heads]
        lbs, cums = [], []
        for h in heads:
            z = zs[h]
            sp = jnp.log(1.0 + jnp.exp2(jnp.abs(z) * -LOG2_E))
            lb = jnp.minimum(z, 0.0) - sp
            l1 = lb - z
            if diag:
                l1 = jnp.where(causal, l1, 0.0)
            hi_f = lax.bitcast_convert_type(
                lax.bitcast_convert_type(l1, jnp.int32) & jnp.int32(-65536), F32)
            cum = _dot(hi_f.astype(BF16), tri) + _dot((l1 - hi_f).astype(BF16), tri)
            cums.append(cum)
            lbs.append(lb + rs[h])
            rs[h] = rs[h] + (cum[:, 0:1] + l1[:, 0:1])
        for h in heads:
            p = h // 2
            a = jnp.exp(lbs[h] + cums[h])
            if diag:
                a = jnp.where(causal, a, 0.0)
            v = v_ref[pl.ds(pl.multiple_of(j * t, t), t), pair_cols(p)]
            accs[p] = accs[p] + _dot(a.astype(BF16), jnp.where(hi_half if h % 2 else lo_half, v, zero_b))
        return tuple(accs), tuple(rs)

    def r_max(rs):
        m = rs[0]
        for r in rs[1:]:
            m = jnp.maximum(m, r)
        return jnp.max(m)

    accs = tuple(jnp.zeros((t, LANES), F32) for _ in range(STICK_PAIRS))
    rs = tuple(jnp.zeros((t, 1), F32) for _ in range(2 * STICK_PAIRS))
    accs, rs = block(i, accs, rs, True)

    def cond(c):
        return jnp.logical_and(c[0] >= 0, c[1] > STICK_DEAD_LOG)

    def body(c):
        j, _, accs, rs = c
        accs, rs = block(j, accs, rs, False)
        return j - 1, r_max(rs), accs, rs

    _, _, accs, _ = lax.while_loop(cond, body, (i - 1, r_max(rs), accs, rs))
    for p in range(STICK_PAIRS):
        o_ref[:, pair_cols(p)] = accs[p]


def _stick_breaking(nat, tr, batch, seq):
    t = T_BLK
    nblk = seq // t
    w = STICK_PAIRS * LANES
    ngroups = AB_WIDTH // w
    n = batch * seq
    return pl.pallas_call(
        _stick_kernel,
        out_shape=jax.ShapeDtypeStruct((n, AB_WIDTH), F32),
        grid=(batch, ngroups, nblk),
        in_specs=[pl.BlockSpec((t, w), lambda b, p, i: (b * nblk + i, p)),
                  pl.BlockSpec((1, nblk, w, t), lambda b, p, i: (b, 0, p, 0)),
                  pl.BlockSpec((seq, w), lambda b, p, i: (b, ngroups + p))],
        out_specs=pl.BlockSpec((t, w), lambda b, p, i: (b * nblk + i, p)),
        compiler_params=_params(("parallel", "parallel", "arbitrary")),
        name="stick_breaking",
    )(nat, tr, nat)


COUNT_ROWS = 32
COUNT_CHUNK = 128
SUM_ROWS = 16


def _key_to_float(key):
    bits = key ^ ((key >> 31) & jnp.int32(0x7FFFFFFF))
    return lax.bitcast_convert_type(bits, F32)


def _dsa_kernel(kn_ref, vT_ref, qT_ref, wi_ref, o_ref, sc_ref, bias_ref, s0_ref, s1_ref,
                *, topk, seq_len):
    t = T_BLK
    i = pl.program_id(1)
    nblk = i + 1
    q_all = qT_ref[0, 0]
    wi = wi_ref[0, 0]
    tq = i * t + lax.broadcasted_iota(jnp.int32, (1, t), 1)
    limit = ((tq >> 6) + 1) << 6
    kcount = jnp.minimum(limit, topk).astype(F32)
    krow = lax.broadcasted_iota(jnp.int32, (t, t), 0)
    kcol = lax.broadcasted_iota(jnp.int32, (t, t), 1)
    qi_off = AB_WIDTH

    def rows(j):
        return pl.ds(pl.multiple_of(j * t, t), t)

    def score_body(j, _):
        ki = kn_ref[rows(j), AB_WIDTH:AB_WIDTH + IDX_DIM]
        dots = [_dot(ki, q_all[qi_off + h * IDX_DIM:qi_off + (h + 1) * IDX_DIM])
                for h in range(IDX_HEADS)]
        sc = wi[0:1] * jnp.maximum(dots[0], 0.0)
        for h in range(1, IDX_HEADS):
            sc = sc + wi[h:h + 1] * jnp.maximum(dots[h], 0.0)
        sc_ref[rows(j), :] = jnp.where(j * t + krow < limit, sc, -jnp.inf)
        return 0

    lax.fori_loop(0, nblk, score_body, 0)

    @pl.when(nblk % 2 == 1)
    def _():
        sc_ref[rows(nblk), :] = jnp.full((t, t), -jnp.inf, F32)

    def count(pred):
        def body(j, c):
            base = pl.multiple_of(j * 2 * t, 2 * t)
            for k in range(2 * t // COUNT_CHUNK):
                s = sc_ref[pl.ds(base + k * COUNT_CHUNK, COUNT_CHUNK), :]
                ind = jnp.where(pred(s), 1.0, 0.0)
                c = c + jnp.sum(ind.reshape(COUNT_CHUNK // COUNT_ROWS, COUNT_ROWS, t), axis=0)
            return c
        c = lax.fori_loop(0, (nblk + 1) // 2, body, jnp.zeros((COUNT_ROWS, t), F32))
        return jnp.sum(c, axis=0, keepdims=True)

    def search_body(it, carry):
        p, n_ge = carry
        cand = p + lax.shift_left(jnp.int32(1), 31 - it)
        cf = _key_to_float(cand)
        cnt = count(lambda s: s >= cf)
        ok = cnt >= kcount
        return jnp.where(ok, cand, p), jnp.where(ok, cnt, n_ge)

    p, n_ge = lax.fori_loop(
        0, 32, search_body,
        (jnp.full((1, t), jnp.iinfo(jnp.int32).min, jnp.int32), jnp.full((1, t), 2.0 * seq_len, F32)))
    thr = _key_to_float(p)

    def mask_exact():
        def body(j, _):
            bias_ref[rows(j), :] = jnp.where(sc_ref[rows(j), :] >= thr, 0.0, NEG_BIG)
            return 0
        lax.fori_loop(0, nblk, body, 0)

    def mask_ties():
        n_ties = kcount - count(lambda s: s > thr)
        tri = jnp.where(kcol < krow, 1.0, 0.0).astype(BF16)

        def body(j, seen):
            s = sc_ref[rows(j), :]
            tie = s == thr
            tie_f = jnp.where(tie, 1.0, 0.0)
            before = _dot(tri, tie_f.astype(BF16)) + seen
            keep_tie = jnp.where(before < n_ties, 0.0, NEG_BIG)
            bias_ref[rows(j), :] = jnp.where(s > thr, 0.0, jnp.where(tie, keep_tie, NEG_BIG))
            return seen + jnp.sum(tie_f, axis=0, keepdims=True)

        lax.fori_loop(0, nblk, body, jnp.zeros((1, t), F32))

    lax.cond(jnp.max(n_ge - kcount) > 0.0, mask_ties, mask_exact)

    @pl.when(nblk % 2 == 1)
    def _():
        bias_ref[rows(nblk), :] = jnp.full((t, t), NEG_BIG, F32)

    lane = lax.broadcasted_iota(jnp.int32, (t, LANES), 1)
    zero_b = jnp.zeros((), BF16)
    npairs = AB_WIDTH // LANES
    sel_k = (lane < HEAD_DIM, lane >= HEAD_DIM)

    def put_logits(buf, j):
        jc = jnp.minimum(j, nblk - 1)
        for pr in range(npairs):
            cols = slice(pr * LANES, (pr + 1) * LANES)
            kb = kn_ref[rows(jc), cols]
            for hh in range(2):
                buf[2 * pr + hh] = _dot(jnp.where(sel_k[hh], kb, zero_b), q_all[cols])

    ones_rows = jnp.ones((SUM_ROWS, t), BF16)

    def consume(buf, j, carry):
        ms, accs = (list(c) for c in carry)
        jc = jnp.minimum(j, nblk - 1)
        bias = bias_ref[rows(j), :]
        for h in range(nh):
            s = buf[h] + bias
            m_new = jnp.maximum(ms[h], jnp.max(s, axis=0, keepdims=True))
            alpha = jnp.exp(ms[h] - m_new)
            pm = jnp.exp(s - m_new).astype(BF16)
            ms[h] = m_new
            v_h = vT_ref[0, jc, h * HEAD_DIM:(h + 1) * HEAD_DIM, :]
            accs[h] = accs[h] * alpha + _dot(jnp.concatenate([v_h, ones_rows], axis=0), pm)
        return tuple(ms), tuple(accs)

    def att_body(jj, carry):
        j0 = 2 * jj
        put_logits(s1_ref, j0 + 1)
        carry = consume(s0_ref, j0, carry)
        put_logits(s0_ref, j0 + 2)
        return consume(s1_ref, j0 + 1, carry)

    nh = 2 * npairs
    init = (tuple(jnp.full((1, t), NEG_BIG, F32) for _ in range(nh)),
            tuple(jnp.zeros((HEAD_DIM + SUM_ROWS, t), F32) for _ in range(nh)))
    put_logits(s0_ref, 0)
    _, accs = lax.fori_loop(0, (nblk + 1) // 2, att_body, init)
    for pr in range(npairs):
        halves = [accs[h][:HEAD_DIM] / accs[h][HEAD_DIM:HEAD_DIM + 1] for h in (2 * pr, 2 * pr + 1)]
        o_ref[:, pr * LANES:(pr + 1) * LANES] = jnp.concatenate(halves, axis=0).T


def _dsa(kn, tr, q_t, wi_t, batch, seq, topk):
    t = T_BLK
    nblk = seq // t
    n = batch * seq
    ckn = kn.shape[1]
    cq = q_t.shape[2]
    return pl.pallas_call(
        functools.partial(_dsa_kernel, topk=topk, seq_len=seq),
        out_shape=jax.ShapeDtypeStruct((n, AB_WIDTH), F32),
        grid=(batch, nblk),
        in_specs=[pl.BlockSpec((seq, ckn), lambda b, i: (b, 0)),
                  pl.BlockSpec((1, nblk, AB_WIDTH, t), lambda b, i: (b, 0, 1, 0)),
                  pl.BlockSpec((1, 1, cq, t), lambda b, i: (b, i, 0, 0)),
                  pl.BlockSpec((1, 1, 8, t), lambda b, i: (b, i, 0, 0))],
        out_specs=pl.BlockSpec((t, AB_WIDTH), lambda b, i: (b * nblk + i, 0)),
        scratch_shapes=[pltpu.VMEM((seq + t, t), F32), pltpu.VMEM((seq + t, t), F32),
                        pltpu.VMEM((AB_WIDTH // HEAD_DIM, t, t), F32),
                        pltpu.VMEM((AB_WIDTH // HEAD_DIM, t, t), F32)],
        compiler_params=_params(("parallel", "arbitrary")),
        name="dsa",
    )(kn, tr, q_t, wi_t)


def _sigmoid(x):
    return 1.0 / (1.0 + jnp.exp(-x))


def _final_kernel(x_ref, h_ref, ya_ref, yb_ref, mkv_ref, wz_ref, wg_ref, wqc_ref, bg_ref,
                  wa_ref, wb_ref, wc_ref, wo_ref, pg_ref, o_ref):
    h = h_ref[...]
    w = AB_WIDTH

    def branch(y, zi, w_ref, gi):
        z = _dot(h, wz_ref[:, zi * w:(zi + 1) * w])
        u = (y * (z * _sigmoid(z))).astype(BF16)
        g = _sigmoid(_dot(h, wg_ref[:, gi * D_MODEL:(gi + 1) * D_MODEL])
                     + bg_ref[:, gi * D_MODEL:(gi + 1) * D_MODEL])
        return g * _dot(u, w_ref[...])

    merged = branch(ya_ref[...], 0, wa_ref, 0)
    merged = merged + branch(yb_ref[...], 1, wb_ref, 1)

    qc = _dot(h, wqc_ref[...]).astype(BF16)
    mkv = mkv_ref[0]
    scale = MEM_HEAD_DIM ** -0.5
    heads = []
    for hd in range(MEM_HEADS):
        sl = slice(hd * MEM_HEAD_DIM, (hd + 1) * MEM_HEAD_DIM)
        logits = lax.dot_general(qc[:, sl], mkv[:, sl], _NT_DIMS,
                                 preferred_element_type=F32) * scale
        mx = jnp.max(logits, axis=-1, keepdims=True)
        e = jnp.exp(logits - mx)
        pm = e / jnp.sum(e, axis=-1, keepdims=True)
        heads.append(_dot(pm.astype(BF16), mkv[:, MEM_WIDTH + hd * MEM_HEAD_DIM:
                                                MEM_WIDTH + (hd + 1) * MEM_HEAD_DIM]))
    yc = jnp.concatenate(heads, axis=-1)
    merged = merged + branch(yc, 2, wc_ref, 2)

    out = _dot(merged.astype(BF16), wo_ref[...])
    ms = jnp.mean(out * out, axis=-1, keepdims=True)
    o_ref[...] = x_ref[...] + out * lax.rsqrt(ms + RMS_EPS) * pg_ref[...]


def _final(x2d, h, ya, yb, mkv, wz, wg, wqc, bg, wa, wb, wc, wo, pg, seq, tm=256):
    n, d = x2d.shape
    per_b = seq // tm

    def const(a):
        nd = a.ndim
        return pl.BlockSpec(a.shape, lambda i: (0,) * nd, pipeline_mode=pl.Buffered(1))

    def rows(width):
        return pl.BlockSpec((tm, width), lambda i: (i, 0))

    return pl.pallas_call(
        _final_kernel,
        out_shape=jax.ShapeDtypeStruct((n, d), F32),
        grid=(n // tm,),
        in_specs=[rows(d), rows(d), rows(AB_WIDTH), rows(AB_WIDTH),
                  pl.BlockSpec((1,) + mkv.shape[1:], lambda i: (i // per_b, 0, 0)),
                  const(wz), const(wg), const(wqc), const(bg),
                  const(wa), const(wb), const(wc), const(wo), const(pg)],
        out_specs=rows(d),
        compiler_params=_params(("parallel",)),
        name="final",
    )(x2d, h, ya, yb, mkv, wz, wg, wqc, bg, wa, wb, wc, wo, pg)


def _rotate_half_cols(w):
    d, c = w.shape
    w4 = w.reshape(d, c // HEAD_DIM, 2, HALF)
    return jnp.stack([-w4[:, :, 1], w4[:, :, 0]], axis=2).reshape(d, c)


def kernel(x, mem, positions, pre_g, post_g, mem_g, w_in, b_gate, w_mem_kv,
           w_br_a, w_br_b, w_br_c, w_out):
    batch, seq, d = x.shape
    n = batch * seq
    depth = w_in.shape[0]
    topk = min(MAX_TOPK, seq // 4)
    aw = AB_WIDTH
    sizes = (aw, aw, aw, aw, aw, aw, aw, aw, IDX_HEADS * IDX_DIM, IDX_DIM, IDX_HEADS,
             MEM_WIDTH, MEM_WIDTH, N_BRANCHES * D_MODEL)
    offs = np.concatenate([[0], np.cumsum(sizes)])

    inv = (ROPE_THETA ** (-np.arange(HALF, dtype=np.float32) / HALF)).astype(np.float32)
    pos_rep = jnp.repeat(positions.reshape(n // 4, 4), HALF, axis=1)
    cos_c, sin_c = _rope_tables_nat(pos_rep, jnp.asarray(np.tile(inv, 4)).reshape(1, LANES))
    cos_n = jnp.tile(cos_c.reshape(n, HALF), (1, LANES // HALF))
    sin_n = jnp.tile(sin_c.reshape(n, HALF), (1, LANES // HALF))
    cos_t, sin_t = _rope_tables_tr(positions.reshape(batch, 1, seq), jnp.asarray(inv).reshape(HALF, 1))

    x2d = x.reshape(n, d)
    for layer in range(depth):
        w = w_in[layer]
        col = lambda k: w[:, offs[k]:offs[k + 1]]
        (w_qa, w_ka, w_va, w_za, w_qb, w_kb, w_vb, w_zb, w_qi, w_ki, w_wi, w_qc, w_zc, w_gt) = (
            col(k) for k in range(14))
        scale = HEAD_DIM ** -0.5
        w_nat = jnp.concatenate([w_qa * scale, w_va], axis=1).astype(BF16)
        w_kn = jnp.concatenate([w_kb, w_ki, jnp.zeros((d, LANES - IDX_DIM), F32)], axis=1)
        w_kn_rot = _rotate_half_cols(w_kn).astype(BF16)
        w_kn = w_kn.astype(BF16)
        w_tr = jnp.concatenate([w_ka, w_vb], axis=1).T.astype(BF16)
        w_qt = jnp.concatenate([w_qb * scale, w_qi * (IDX_DIM ** -0.5), w_wi * (IDX_HEADS ** -0.5),
                                jnp.zeros((d, 8 - IDX_HEADS), F32)], axis=1).T.astype(BF16)
        w_z = jnp.concatenate([w_za, w_zb, w_zc], axis=1).astype(BF16)

        h = _rmsnorm_bf16(x2d, pre_g[layer])
        nat = _proj_nat(h, w_nat)
        kn = _proj_nat_rope(h, w_kn, w_kn_rot, cos_n, sin_n)
        tr = _proj_tr(h, w_tr, batch, seq)
        q_t, wi_t = _proj_tr_rope(h, w_qt, cos_t, sin_t, batch, seq,
                                  (aw + IDX_HEADS * IDX_DIM) // HEAD_DIM)
        mkv = _memkv(mem, mem_g[layer], w_mem_kv[layer].astype(BF16))

        ya = _stick_breaking(nat, tr, batch, seq)
        yb = _dsa(kn, tr, q_t, wi_t, batch, seq, topk)
        x2d = _final(x2d, h, ya, yb, mkv, w_z, w_gt.astype(BF16), w_qc.astype(BF16),
                     b_gate[layer].reshape(1, -1), w_br_a[layer].astype(BF16),
                     w_br_b[layer].astype(BF16), w_br_c[layer].astype(BF16),
                     w_out[layer].astype(BF16), post_g[layer].reshape(1, d), seq)
    return x2d.reshape(batch, seq, d)
```

```python
import functools

import numpy as np
import jax
import jax.numpy as jnp
from jax import lax
from jax.experimental import pallas as pl
from jax.experimental.pallas import tpu as pltpu

F32 = jnp.float32
BF16 = jnp.bfloat16

D_MODEL = 1024
CHUNK = 64
ROPE_THETA = 10000.0
RMS_EPS = 1e-6
HEAD_DIM = 64
HALF = HEAD_DIM // 2
AB_WIDTH = 512
IDX_HEADS = 4
IDX_DIM = 64
MAX_TOPK = 256
MEM_HEADS = 4
MEM_HEAD_DIM = 128
MEM_WIDTH = MEM_HEADS * MEM_HEAD_DIM
N_BRANCHES = 3

LANES = 128
T_BLK = 256
NEG_BIG = -1e30
LOG2_E = 1.4426950408889634
VMEM_LIMIT = 56 * 1024 * 1024

_NT_DIMS = (((1,), (1,)), ((), ()))


def _dot(a, b):
    return jnp.dot(a, b, preferred_element_type=F32)


def _params(sem, vmem=VMEM_LIMIT):
    return pltpu.CompilerParams(dimension_semantics=sem, vmem_limit_bytes=vmem)


def _rms_kernel(x_ref, g_ref, o_ref):
    x = x_ref[...]
    ms = jnp.mean(x * x, axis=-1, keepdims=True)
    o_ref[...] = (x * lax.rsqrt(ms + RMS_EPS) * g_ref[...]).astype(o_ref.dtype)


def _rmsnorm_bf16(x2d, g, tm=512):
    n, d = x2d.shape
    return pl.pallas_call(
        _rms_kernel,
        out_shape=jax.ShapeDtypeStruct((n, d), BF16),
        grid=(n // tm,),
        in_specs=[pl.BlockSpec((tm, d), lambda i: (i, 0)),
                  pl.BlockSpec((1, d), lambda i: (0, 0))],
        out_specs=pl.BlockSpec((tm, d), lambda i: (i, 0)),
        compiler_params=_params(("parallel",)),
        name="rmsnorm",
    )(x2d, g.reshape(1, d))


def _rope_table_t_kernel(pos_ref, inv_ref, cos_ref, sin_ref):
    ang = pos_ref[0].astype(F32) * inv_ref[...]
    cos_ref[0] = jnp.cos(ang)
    sin_ref[0] = jnp.sin(ang)


def _rope_tables_tr(pos3, inv_col):
    b, _, s = pos3.shape
    ts = min(512, s)
    out = jax.ShapeDtypeStruct((b, HALF, s), F32)
    return pl.pallas_call(
        _rope_table_t_kernel,
        out_shape=(out, out),
        grid=(b, s // ts),
        in_specs=[pl.BlockSpec((1, 1, ts), lambda i, j: (i, 0, j)),
                  pl.BlockSpec((HALF, 1), lambda i, j: (0, 0))],
        out_specs=(pl.BlockSpec((1, HALF, ts), lambda i, j: (i, 0, j)),
                   pl.BlockSpec((1, HALF, ts), lambda i, j: (i, 0, j))),
        compiler_params=_params(("parallel", "parallel")),
        name="rope_tables_tr",
    )(pos3, inv_col)


def _proj_nat_kernel(h_ref, w_ref, o_ref):
    o_ref[...] = _dot(h_ref[...], w_ref[...]).astype(o_ref.dtype)


def _proj_nat(h, w, tm=512):
    n, d = h.shape
    c = w.shape[1]
    return pl.pallas_call(
        _proj_nat_kernel,
        out_shape=jax.ShapeDtypeStruct((n, c), BF16),
        grid=(n // tm,),
        in_specs=[pl.BlockSpec((tm, d), lambda i: (i, 0)),
                  pl.BlockSpec((d, c), lambda i: (0, 0))],
        out_specs=pl.BlockSpec((tm, c), lambda i: (i, 0)),
        compiler_params=_params(("parallel",)),
        name="proj_nat",
    )(h, w)


def _proj_tr_kernel(wt_ref, h_ref, o_ref):
    r = lax.dot_general(wt_ref[...], h_ref[...], _NT_DIMS, preferred_element_type=F32)
    o_ref[0, 0] = r.astype(o_ref.dtype)


def _proj_tr(h, wt, batch, seq):
    c, d = wt.shape
    nblk = seq // T_BLK
    return pl.pallas_call(
        _proj_tr_kernel,
        out_shape=jax.ShapeDtypeStruct((batch, nblk, c, T_BLK), BF16),
        grid=(batch, nblk),
        in_specs=[pl.BlockSpec((c, d), lambda b, i: (0, 0)),
                  pl.BlockSpec((T_BLK, d), lambda b, i: (b * nblk + i, 0))],
        out_specs=pl.BlockSpec((1, 1, c, T_BLK), lambda b, i: (b, i, 0, 0)),
        compiler_params=_params(("parallel", "parallel")),
        name="proj_tr",
    )(wt, h)


def _proj_tr_rope_kernel(wt_ref, h_ref, cos_ref, sin_ref, q_ref, kn_ref, w_ref, *, q_heads, k_heads):
    r = lax.dot_general(wt_ref[...], h_ref[...], _NT_DIMS, preferred_element_type=F32)
    c = cos_ref[0]
    s = sin_ref[0]

    def roped(hd):
        lo = hd * HEAD_DIM
        t1 = r[lo:lo + HALF]
        t2 = r[lo + HALF:lo + HEAD_DIM]
        return t1 * c - t2 * s, t2 * c + t1 * s

    for hd in range(q_heads):
        lo = hd * HEAD_DIM
        first, second = roped(hd)
        q_ref[0, 0, lo:lo + HALF, :] = first.astype(q_ref.dtype)
        q_ref[0, 0, lo + HALF:lo + HEAD_DIM, :] = second.astype(q_ref.dtype)
    for pair in range(k_heads // 2):
        pieces = roped(q_heads + 2 * pair) + roped(q_heads + 2 * pair + 1)
        blk = jnp.concatenate(pieces, axis=0)
        kn_ref[:, pair * LANES:(pair + 1) * LANES] = blk.T.astype(kn_ref.dtype)
    w_ref[0, 0] = r[(q_heads + k_heads) * HEAD_DIM:]


def _proj_tr_rope(h, wt, cos_t, sin_t, batch, seq, q_heads, k_heads):
    c, d = wt.shape
    c_q = q_heads * HEAD_DIM
    c_k = k_heads * HEAD_DIM
    c_rest = c - c_q - c_k
    nblk = seq // T_BLK
    return pl.pallas_call(
        functools.partial(_proj_tr_rope_kernel, q_heads=q_heads, k_heads=k_heads),
        out_shape=(jax.ShapeDtypeStruct((batch, nblk, c_q, T_BLK), BF16),
                   jax.ShapeDtypeStruct((batch * seq, c_k), BF16),
                   jax.ShapeDtypeStruct((batch, nblk, c_rest, T_BLK), F32)),
        grid=(batch, nblk),
        in_specs=[pl.BlockSpec((c, d), lambda b, i: (0, 0)),
                  pl.BlockSpec((T_BLK, d), lambda b, i: (b * nblk + i, 0)),
                  pl.BlockSpec((1, HALF, T_BLK), lambda b, i: (b, 0, i)),
                  pl.BlockSpec((1, HALF, T_BLK), lambda b, i: (b, 0, i))],
        out_specs=(pl.BlockSpec((1, 1, c_q, T_BLK), lambda b, i: (b, i, 0, 0)),
                   pl.BlockSpec((T_BLK, c_k), lambda b, i: (b * nblk + i, 0)),
                   pl.BlockSpec((1, 1, c_rest, T_BLK), lambda b, i: (b, i, 0, 0))),
        compiler_params=_params(("parallel", "parallel")),
        name="proj_tr_rope",
    )(wt, h, cos_t, sin_t)


def _memkv_kernel(mem_ref, g_ref, w_ref, o_ref):
    x = mem_ref[0]
    ms = jnp.mean(x * x, axis=-1, keepdims=True)
    hn = (x * lax.rsqrt(ms + RMS_EPS) * g_ref[...]).astype(BF16)
    o_ref[0] = _dot(hn, w_ref[...]).astype(o_ref.dtype)


def _memkv(mem, g, w):
    b, m, d = mem.shape
    c = w.shape[1]
    return pl.pallas_call(
        _memkv_kernel,
        out_shape=jax.ShapeDtypeStruct((b, m, c), BF16),
        grid=(b,),
        in_specs=[pl.BlockSpec((1, m, d), lambda i: (i, 0, 0)),
                  pl.BlockSpec((1, d), lambda i: (0, 0)),
                  pl.BlockSpec((d, c), lambda i: (0, 0))],
        out_specs=pl.BlockSpec((1, m, c), lambda i: (i, 0, 0)),
        compiler_params=_params(("parallel",)),
        name="mem_kv",
    )(mem, g.reshape(1, d), w)


STICK_DEAD_LOG = -110.0
STICK_PAIRS = 4


def _stick_kernel(q_ref, kT_ref, v_ref, o_ref):
    t = T_BLK
    i = pl.program_id(2)
    lane = lax.broadcasted_iota(jnp.int32, (t, LANES), 1)
    zero_b = jnp.zeros((), BF16)
    lo_half, hi_half = lane < HEAD_DIM, lane >= HEAD_DIM

    def pair_cols(p):
        return slice(p * LANES, (p + 1) * LANES)

    q_heads = []
    for p in range(STICK_PAIRS):
        qp = q_ref[:, pair_cols(p)]
        q_heads += [jnp.where(lo_half, qp, zero_b), jnp.where(hi_half, qp, zero_b)]
    row = lax.broadcasted_iota(jnp.int32, (t, t), 0)
    col = lax.broadcasted_iota(jnp.int32, (t, t), 1)
    tri = jnp.where(row > col, 1.0, 0.0).astype(BF16)
    causal = col < row

    def block(j, accs, rs, diag):
        accs, rs = list(accs), list(rs)
        heads = range(2 * STICK_PAIRS)
        zs = [_dot(q_heads[h], kT_ref[0, j, pair_cols(h // 2), :]) for h in heads]
        lbs, cums = [], []
        for h in heads:
            z = zs[h]
            sp = jnp.log(1.0 + jnp.exp2(jnp.abs(z) * -LOG2_E))
            lb = jnp.minimum(z, 0.0) - sp
            l1 = lb - z
            if diag:
                l1 = jnp.where(causal, l1, 0.0)
            hi_f = lax.bitcast_convert_type(
                lax.bitcast_convert_type(l1, jnp.int32) & jnp.int32(-65536), F32)
            cum = _dot(hi_f.astype(BF16), tri) + _dot((l1 - hi_f).astype(BF16), tri)
            cums.append(cum)
            lbs.append(lb + rs[h])
            rs[h] = rs[h] + (cum[:, 0:1] + l1[:, 0:1])
        for h in heads:
            p = h // 2
            a = jnp.exp(lbs[h] + cums[h])
            if diag:
                a = jnp.where(causal, a, 0.0)
            v = v_ref[pl.ds(pl.multiple_of(j * t, t), t), pair_cols(p)]
            accs[p] = accs[p] + _dot(a.astype(BF16), jnp.where(hi_half if h % 2 else lo_half, v, zero_b))
        return tuple(accs), tuple(rs)

    def r_max(rs):
        m = rs[0]
        for r in rs[1:]:
            m = jnp.maximum(m, r)
        return jnp.max(m)

    accs = tuple(jnp.zeros((t, LANES), F32) for _ in range(STICK_PAIRS))
    rs = tuple(jnp.zeros((t, 1), F32) for _ in range(2 * STICK_PAIRS))
    accs, rs = block(i, accs, rs, True)

    def cond(c):
        return jnp.logical_and(c[0] >= 0, c[1] > STICK_DEAD_LOG)

    def body(c):
        j, _, accs, rs = c
        accs, rs = block(j, accs, rs, False)
        return j - 1, r_max(rs), accs, rs

    _, _, accs, _ = lax.while_loop(cond, body, (i - 1, r_max(rs), accs, rs))
    for p in range(STICK_PAIRS):
        o_ref[:, pair_cols(p)] = accs[p]


def _stick_breaking(nat, tr, batch, seq):
    t = T_BLK
    nblk = seq // t
    w = STICK_PAIRS * LANES
    ngroups = AB_WIDTH // w
    n = batch * seq
    return pl.pallas_call(
        _stick_kernel,
        out_shape=jax.ShapeDtypeStruct((n, AB_WIDTH), F32),
        grid=(batch, ngroups, nblk),
        in_specs=[pl.BlockSpec((t, w), lambda b, p, i: (b * nblk + i, p)),
                  pl.BlockSpec((1, nblk, w, t), lambda b, p, i: (b, 0, p, 0)),
                  pl.BlockSpec((seq, w), lambda b, p, i: (b, ngroups + p))],
        out_specs=pl.BlockSpec((t, w), lambda b, p, i: (b * nblk + i, p)),
        compiler_params=_params(("parallel", "parallel", "arbitrary")),
        name="stick_breaking",
    )(nat, tr, nat)


COUNT_ROWS = 32
COUNT_CHUNK = 128
SUM_ROWS = 16
REFINE_STEPS = 8


def _key_to_float(key):
    bits = key ^ ((key >> 31) & jnp.int32(0x7FFFFFFF))
    return lax.bitcast_convert_type(bits, F32)


def _dsa_kernel(kn_ref, vT_ref, qT_ref, wi_ref, o_ref, sc_ref, bias_ref, s0_ref, s1_ref,
                *, topk, seq_len):
    t = T_BLK
    i = pl.program_id(1)
    nblk = i + 1
    q_all = qT_ref[0, 0]
    wi = wi_ref[0, 0]
    tq = i * t + lax.broadcasted_iota(jnp.int32, (1, t), 1)
    limit = ((tq >> 6) + 1) << 6
    kcount = jnp.minimum(limit, topk).astype(F32)
    krow = lax.broadcasted_iota(jnp.int32, (t, t), 0)
    kcol = lax.broadcasted_iota(jnp.int32, (t, t), 1)
    qi_cat = jnp.concatenate([q_all[AB_WIDTH + h * IDX_DIM:AB_WIDTH + (h + 1) * IDX_DIM]
                              for h in range(IDX_HEADS)], axis=1)

    def rows(j):
        return pl.ds(pl.multiple_of(j * t, t), t)

    n2 = (nblk + 1) // 2

    def score_body(jj, _):
        js = (2 * jj, 2 * jj + 1)
        dots = [_dot(kn_ref[rows(jnp.minimum(j, nblk - 1)), AB_WIDTH:AB_WIDTH + IDX_DIM], qi_cat)
                for j in js]
        for j, d in zip(js, dots):
            sc = wi[0:1] * jnp.maximum(d[:, :t], 0.0)
            for h in range(1, IDX_HEADS):
                sc = sc + wi[h:h + 1] * jnp.maximum(d[:, h * t:(h + 1) * t], 0.0)
            sc_ref[rows(j), :] = jnp.where(j * t + krow < limit, sc, -jnp.inf)
        return 0

    lax.fori_loop(0, n2, score_body, 0)

    def chunks(j):
        base = pl.multiple_of(j * 2 * t, 2 * t)
        return [pl.ds(base + k * COUNT_CHUNK, COUNT_CHUNK) for k in range(2 * t // COUNT_CHUNK)]

    def count(pred):
        def body(j, c):
            for rows_k in chunks(j):
                ind = jnp.where(pred(sc_ref[rows_k, :]), 1.0, 0.0)
                c = c + jnp.sum(ind.reshape(COUNT_CHUNK // COUNT_ROWS, COUNT_ROWS, t), axis=0)
            return c
        c = lax.fori_loop(0, n2, body, jnp.zeros((COUNT_ROWS, t), F32))
        return jnp.sum(c, axis=0, keepdims=True)

    def search_body(it, carry):
        p, n_ge = carry
        cand = p + lax.shift_left(jnp.int32(1), 31 - it)
        cf = _key_to_float(cand)
        cnt = count(lambda s: s >= cf)
        ok = cnt >= kcount
        return jnp.where(ok, cand, p), jnp.where(ok, cnt, n_ge)

    p, n_ge = lax.fori_loop(
        0, 32, search_body,
        (jnp.full((1, t), jnp.iinfo(jnp.int32).min, jnp.int32), jnp.full((1, t), 2.0 * seq_len, F32)))
    thr = _key_to_float(p)

    def mask_exact():
        def body(j, _):
            bias_ref[rows(j), :] = jnp.where(sc_ref[rows(j), :] >= thr, 0.0, NEG_BIG)
            return 0
        lax.fori_loop(0, nblk, body, 0)

    def mask_ties():
        def refine(_, carry):
            lo, hi = carry
            mid = lo + (hi - lo) * 0.5
            ok = count(lambda s: s >= mid) >= kcount
            return jnp.where(ok, mid, lo), jnp.where(ok, hi, mid)

        thr_r, _ = lax.fori_loop(0, REFINE_STEPS, refine, (thr, _key_to_float(p + 1)))
        n_ties = kcount - count(lambda s: s > thr_r)
        tri = jnp.where(kcol < krow, 1.0, 0.0).astype(BF16)

        def body(j, seen):
            s = sc_ref[rows(j), :]
            tie = s == thr_r
            tie_f = jnp.where(tie, 1.0, 0.0)
            before = _dot(tri, tie_f.astype(BF16)) + seen
            keep_tie = jnp.where(before < n_ties, 0.0, NEG_BIG)
            bias_ref[rows(j), :] = jnp.where(s > thr_r, 0.0, jnp.where(tie, keep_tie, NEG_BIG))
            return seen + jnp.sum(tie_f, axis=0, keepdims=True)

        lax.fori_loop(0, nblk, body, jnp.zeros((1, t), F32))

    lax.cond(jnp.max(n_ge - kcount) > 0.0, mask_ties, mask_exact)

    @pl.when(nblk % 2 == 1)
    def _():
        bias_ref[rows(nblk), :] = jnp.full((t, t), NEG_BIG, F32)

    lane = lax.broadcasted_iota(jnp.int32, (t, LANES), 1)
    zero_b = jnp.zeros((), BF16)
    npairs = AB_WIDTH // LANES
    sel_k = (lane < HEAD_DIM, lane >= HEAD_DIM)

    last_written = nblk - 1 + (nblk % 2)

    def put_logits(buf, j):
        jc = jnp.minimum(j, nblk - 1)
        dots = []
        for pr in range(npairs):
            cols = slice(pr * LANES, (pr + 1) * LANES)
            kb = kn_ref[rows(jc), cols]
            dots += [_dot(jnp.where(sel_k[hh], kb, zero_b), q_all[cols]) for hh in range(2)]
        bias = bias_ref[rows(jnp.minimum(j, last_written)), :]
        maxima = []
        for h in range(nh):
            s = dots[h] + bias
            buf[h] = s
            maxima.append(jnp.max(s, axis=0, keepdims=True))
        return tuple(maxima)

    ones_rows = jnp.ones((SUM_ROWS, t), BF16)

    def consume(buf, maxima, j, carry):
        ms, accs = (list(c) for c in carry)
        jc = jnp.minimum(j, nblk - 1)
        for h in range(nh):
            m_new = jnp.maximum(ms[h], maxima[h])
            alpha = jnp.exp(ms[h] - m_new)
            pm = jnp.exp(buf[h] - m_new).astype(BF16)
            ms[h] = m_new
            v_h = vT_ref[0, jc, h * HEAD_DIM:(h + 1) * HEAD_DIM, :]
            accs[h] = accs[h] * alpha + _dot(jnp.concatenate([v_h, ones_rows], axis=0), pm)
        return tuple(ms), tuple(accs)

    def att_body(jj, carry):
        ms, accs, max0 = carry
        j0 = 2 * jj
        max1 = put_logits(s1_ref, j0 + 1)
        ms, accs = consume(s0_ref, max0, j0, (ms, accs))
        max0 = put_logits(s0_ref, j0 + 2)
        ms, accs = consume(s1_ref, max1, j0 + 1, (ms, accs))
        return ms, accs, max0

    nh = 2 * npairs
    init = (tuple(jnp.full((1, t), NEG_BIG, F32) for _ in range(nh)),
            tuple(jnp.zeros((HEAD_DIM + SUM_ROWS, t), F32) for _ in range(nh)),
            put_logits(s0_ref, 0))
    _, accs, _ = lax.fori_loop(0, (nblk + 1) // 2, att_body, init)
    for pr in range(npairs):
        halves = [accs[h][:HEAD_DIM] / accs[h][HEAD_DIM:HEAD_DIM + 1] for h in (2 * pr, 2 * pr + 1)]
        o_ref[:, pr * LANES:(pr + 1) * LANES] = jnp.concatenate(halves, axis=0).T


def _dsa(kn, tr, q_t, wi_t, batch, seq, topk):
    t = T_BLK
    nblk = seq // t
    n = batch * seq
    ckn = kn.shape[1]
    cq = q_t.shape[2]
    return pl.pallas_call(
        functools.partial(_dsa_kernel, topk=topk, seq_len=seq),
        out_shape=jax.ShapeDtypeStruct((n, AB_WIDTH), F32),
        grid=(batch, nblk),
        in_specs=[pl.BlockSpec((seq, ckn), lambda b, i: (b, 0)),
                  pl.BlockSpec((1, nblk, AB_WIDTH, t), lambda b, i: (b, 0, 1, 0)),
                  pl.BlockSpec((1, 1, cq, t), lambda b, i: (b, i, 0, 0)),
                  pl.BlockSpec((1, 1, 8, t), lambda b, i: (b, i, 0, 0))],
        out_specs=pl.BlockSpec((t, AB_WIDTH), lambda b, i: (b * nblk + i, 0)),
        scratch_shapes=[pltpu.VMEM((seq + t, t), F32), pltpu.VMEM((seq + t, t), F32),
                        pltpu.VMEM((AB_WIDTH // HEAD_DIM, t, t), F32),
                        pltpu.VMEM((AB_WIDTH // HEAD_DIM, t, t), F32)],
        compiler_params=_params(("parallel", "arbitrary")),
        name="dsa",
    )(kn, tr, q_t, wi_t)


def _sigmoid(x):
    return 1.0 / (1.0 + jnp.exp(-x))


def _final_kernel(x_ref, h_ref, ya_ref, yb_ref, mkv_ref, wz_ref, wg_ref, wqc_ref, bg_ref,
                  wa_ref, wb_ref, wc_ref, wo_ref, pg_ref, o_ref):
    h = h_ref[...]
    w = AB_WIDTH

    def branch(y, zi, w_ref, gi):
        z = _dot(h, wz_ref[:, zi * w:(zi + 1) * w])
        u = (y * (z * _sigmoid(z))).astype(BF16)
        g = _sigmoid(_dot(h, wg_ref[:, gi * D_MODEL:(gi + 1) * D_MODEL])
                     + bg_ref[:, gi * D_MODEL:(gi + 1) * D_MODEL])
        return g * _dot(u, w_ref[...])

    merged = branch(ya_ref[...], 0, wa_ref, 0)
    merged = merged + branch(yb_ref[...], 1, wb_ref, 1)

    qc = _dot(h, wqc_ref[...]).astype(BF16)
    mkv = mkv_ref[0]
    scale = MEM_HEAD_DIM ** -0.5
    heads = []
    for hd in range(MEM_HEADS):
        sl = slice(hd * MEM_HEAD_DIM, (hd + 1) * MEM_HEAD_DIM)
        logits = lax.dot_general(qc[:, sl], mkv[:, sl], _NT_DIMS,
                                 preferred_element_type=F32) * scale
        mx = jnp.max(logits, axis=-1, keepdims=True)
        e = jnp.exp(logits - mx)
        pm = e / jnp.sum(e, axis=-1, keepdims=True)
        heads.append(_dot(pm.astype(BF16), mkv[:, MEM_WIDTH + hd * MEM_HEAD_DIM:
                                                MEM_WIDTH + (hd + 1) * MEM_HEAD_DIM]))
    yc = jnp.concatenate(heads, axis=-1)
    merged = merged + branch(yc, 2, wc_ref, 2)

    out = _dot(merged.astype(BF16), wo_ref[...])
    ms = jnp.mean(out * out, axis=-1, keepdims=True)
    o_ref[...] = x_ref[...] + out * lax.rsqrt(ms + RMS_EPS) * pg_ref[...]


def _final(x2d, h, ya, yb, mkv, wz, wg, wqc, bg, wa, wb, wc, wo, pg, seq, tm=256):
    n, d = x2d.shape
    per_b = seq // tm

    def const(a):
        nd = a.ndim
        return pl.BlockSpec(a.shape, lambda i: (0,) * nd, pipeline_mode=pl.Buffered(1))

    def rows(width):
        return pl.BlockSpec((tm, width), lambda i: (i, 0))

    return pl.pallas_call(
        _final_kernel,
        out_shape=jax.ShapeDtypeStruct((n, d), F32),
        grid=(n // tm,),
        in_specs=[rows(d), rows(d), rows(AB_WIDTH), rows(AB_WIDTH),
                  pl.BlockSpec((1,) + mkv.shape[1:], lambda i: (i // per_b, 0, 0)),
                  const(wz), const(wg), const(wqc), const(bg),
                  const(wa), const(wb), const(wc), const(wo), const(pg)],
        out_specs=rows(d),
        compiler_params=_params(("parallel",)),
        name="final",
    )(x2d, h, ya, yb, mkv, wz, wg, wqc, bg, wa, wb, wc, wo, pg)


def kernel(x, mem, positions, pre_g, post_g, mem_g, w_in, b_gate, w_mem_kv,
           w_br_a, w_br_b, w_br_c, w_out):
    batch, seq, d = x.shape
    n = batch * seq
    depth = w_in.shape[0]
    topk = min(MAX_TOPK, seq // 4)
    aw = AB_WIDTH
    sizes = (aw, aw, aw, aw, aw, aw, aw, aw, IDX_HEADS * IDX_DIM, IDX_DIM, IDX_HEADS,
             MEM_WIDTH, MEM_WIDTH, N_BRANCHES * D_MODEL)
    offs = np.concatenate([[0], np.cumsum(sizes)])

    assert seq % (2 * T_BLK) == 0 and d == D_MODEL, (seq, d)
    inv = ROPE_THETA ** (-jnp.arange(HALF, dtype=F32) / HALF)
    cos_t, sin_t = _rope_tables_tr(positions.reshape(batch, 1, seq), inv.reshape(HALF, 1))

    x2d = x.reshape(n, d)
    for layer in range(depth):
        w = w_in[layer]
        col = lambda k: w[:, offs[k]:offs[k + 1]]
        (w_qa, w_ka, w_va, w_za, w_qb, w_kb, w_vb, w_zb, w_qi, w_ki, w_wi, w_qc, w_zc, w_gt) = (
            col(k) for k in range(14))
        scale = HEAD_DIM ** -0.5
        w_nat = jnp.concatenate([w_qa * scale, w_va], axis=1).astype(BF16)
        w_tr = jnp.concatenate([w_ka, w_vb], axis=1).T.astype(BF16)
        w_qt = jnp.concatenate([w_qb * scale, w_qi * (IDX_DIM ** -0.5),
                                w_kb, w_ki, jnp.zeros((d, LANES - IDX_DIM), F32),
                                w_wi * (IDX_HEADS ** -0.5), jnp.zeros((d, 8 - IDX_HEADS), F32)],
                               axis=1).T.astype(BF16)
        w_z = jnp.concatenate([w_za, w_zb, w_zc], axis=1).astype(BF16)

        h = _rmsnorm_bf16(x2d, pre_g[layer])
        nat = _proj_nat(h, w_nat)
        tr = _proj_tr(h, w_tr, batch, seq)
        q_t, kn, wi_t = _proj_tr_rope(h, w_qt, cos_t, sin_t, batch, seq,
                                      (aw + IDX_HEADS * IDX_DIM) // HEAD_DIM,
                                      (aw + LANES) // HEAD_DIM)
        mkv = _memkv(mem, mem_g[layer], w_mem_kv[layer].astype(BF16))

        ya = _stick_breaking(nat, tr, batch, seq)
        yb = _dsa(kn, tr, q_t, wi_t, batch, seq, topk)
        x2d = _final(x2d, h, ya, yb, mkv, w_z, w_gt.astype(BF16), w_qc.astype(BF16),
                     b_gate[layer].reshape(1, -1), w_br_a[layer].astype(BF16),
                     w_br_b[layer].astype(BF16), w_br_c[layer].astype(BF16),
                     w_out[layer].astype(BF16), post_g[layer].reshape(1, d), seq)
    return x2d.reshape(batch, seq, d)
```
